```python
import math
import jax, jax.numpy as jnp
from jax import lax
import numpy as np

D_MODEL = 1024
BATCH = 16
SEQ = 4096
DEPTH = 4

D_MIX = D_MODEL
ATT_HEADS = 8
ATT_HEAD_DIM = 64
ATT_WIDTH = ATT_HEADS * ATT_HEAD_DIM
KV_LATENT = 128
IDX_HEADS = 8
IDX_DIM = 32
TOPK_MAX = 256
Q_BLOCK = 128
SSM_WIDTH = D_MIX - ATT_WIDTH
GROUP_CH = 16
N_GROUPS = SSM_WIDTH // GROUP_CH
STATE = 64
DT_MIN = 0.001
DT_MAX = 0.1
LN_EPS = 1e-5
RMS_EPS = 1e-6
DEEPNORM_ALPHA = (2 * DEPTH) ** 0.25
DEEPNORM_BETA = (8 * DEPTH) ** -0.25
SPLITS = (ATT_WIDTH, KV_LATENT, IDX_HEADS * IDX_DIM, IDX_DIM, IDX_HEADS, ATT_WIDTH, SSM_WIDTH, SSM_WIDTH)
D_IN = sum(SPLITS)

kernel_name = 'hymba_dsa_s5_deepnorm_trunk'


def _split_points():
    pts, acc = [], 0
    for s in SPLITS[:-1]:
        acc += s
        pts.append(acc)
    return pts


def layer_norm(x, g, b):
    xf = x.astype(jnp.float32)
    mu = jnp.mean(xf, axis=-1, keepdims=True)
    var = jnp.mean(jnp.square(xf - mu), axis=-1, keepdims=True)
    y = (xf - mu) * lax.rsqrt(var + LN_EPS) * g.astype(jnp.float32) + b.astype(jnp.float32)
    return y.astype(x.dtype)


def rms_norm(x, g):
    xf = x.astype(jnp.float32)
    y = xf * lax.rsqrt(jnp.mean(jnp.square(xf), axis=-1, keepdims=True) + RMS_EPS)
    return y * g.astype(jnp.float32)


def dsa_attention(q, c_kv, q_idx, k_idx, w_idx, kv_g, w_uk, w_uv):
    bsz, seq, _ = q.shape
    n_sel = min(TOPK_MAX, seq // 4)
    n_blk = seq // Q_BLOCK
    c = rms_norm(c_kv, kv_g)
    qh = q.reshape(bsz, seq, ATT_HEADS, ATT_HEAD_DIM)
    q_lat = jnp.einsum('bshd,hcd->bshc', qh, w_uk).astype(jnp.float32) * (ATT_HEAD_DIM ** -0.5)
    qi = q_idx.reshape(bsz, seq, IDX_HEADS, IDX_DIM).astype(jnp.float32) * (IDX_DIM ** -0.5)
    ki = k_idx.astype(jnp.float32)
    wi = w_idx.astype(jnp.float32) * (IDX_HEADS ** -0.5)
    key_pos = jnp.arange(seq, dtype=jnp.int32)
    neg = jnp.finfo(jnp.float32).min

    def to_blocks(a):
        return jnp.moveaxis(a.reshape(bsz, n_blk, Q_BLOCK, *a.shape[2:]), 1, 0)

    def block(args):
        blk, ql_b, qi_b, wi_b = args
        q_pos = blk * Q_BLOCK + jnp.arange(Q_BLOCK, dtype=jnp.int32)
        causal = key_pos[None, :] <= q_pos[:, None]
        idx_logits = jnp.einsum('bthd,bsd->bths', qi_b, ki)
        index_score = jnp.einsum('bths,bth->bts', jax.nn.relu(idx_logits), wi_b)
        index_score = jnp.where(causal[None], index_score, neg)
        _, sel = lax.top_k(index_score, n_sel)
        valid = sel <= q_pos[None, :, None]
        c_sel = jax.vmap(lambda cb, ib: cb[ib])(c, sel)
        s = jnp.einsum('bthc,btkc->bthk', ql_b, c_sel)
        s = jnp.where(valid[:, :, None, :], s, neg)
        p = jax.nn.softmax(s, axis=-1)
        return jnp.einsum('bthk,btkc->bthc', p, c_sel)

    o_lat = lax.map(block, (jnp.arange(n_blk, dtype=jnp.int32), to_blocks(q_lat), to_blocks(qi), to_blocks(wi)))
    o_lat = jnp.moveaxis(o_lat, 0, 1).reshape(bsz, seq, ATT_HEADS, KV_LATENT)
    out = jnp.einsum('bshc,hcd->bshd', o_lat.astype(w_uv.dtype), w_uv)
    return out.reshape(bsz, seq, ATT_WIDTH).astype(q.dtype)


def _ssm_combine(e_i, e_j):
    a_i, b_i = e_i
    a_j, b_j = e_j
    return a_j * a_i, a_j * b_i + b_j


def s5_branch(u, log_dt, a_re, a_im, b_re, b_im, c_re, c_im, d_skip, w_glu, b_glu):
    bsz, seq, _ = u.shape
    f32 = jnp.float32
    u32 = u.astype(f32).reshape(bsz, seq, N_GROUPS, GROUP_CH)
    lam = lax.complex(a_re.astype(f32), a_im.astype(f32))
    dt = jnp.exp(log_dt.astype(f32))[:, None]
    a_bar = jnp.exp(lam * dt)
    b_bar = ((a_bar - 1.0) / lam)[..., None] * lax.complex(b_re.astype(f32), b_im.astype(f32))
    c_mat = lax.complex(c_re.astype(f32), c_im.astype(f32))
    bu = jnp.einsum('bsgc,gpc->bsgp', u32.astype(jnp.complex64), b_bar)
    a_seq = jnp.broadcast_to(a_bar[None, None], (1, seq, N_GROUPS, STATE))
    _, states = lax.associative_scan(_ssm_combine, (a_seq, bu), axis=1)
    y = jnp.einsum('bsgp,gcp->bsgc', states, c_mat).real + d_skip.astype(f32) * u32
    y = jax.nn.gelu(y.reshape(bsz, seq, SSM_WIDTH))
    h = jnp.einsum('bsc,ce->bse', y, w_glu.astype(f32)) + b_glu.astype(f32)
    val, gate = jnp.split(h, 2, axis=-1)
    return (val * jax.nn.sigmoid(gate)).astype(u.dtype)


def hybrid_layer(x, w_in, kv_g, w_uk, w_uv, log_dt, a_re, a_im, b_re, b_im, c_re, c_im, d_skip, w_glu, b_glu, w_out, ln_g, ln_b):
    proj = jnp.einsum('bsd,de->bse', x, w_in)
    q, c_kv, q_idx, k_idx, w_idx, gate_a, u, gate_s = jnp.split(proj, _split_points(), axis=-1)
    att = dsa_attention(q, c_kv, q_idx, k_idx, w_idx, kv_g, w_uk, w_uv)
    ssm = s5_branch(u, log_dt, a_re, a_im, b_re, b_im, c_re, c_im, d_skip, w_glu, b_glu)
    mixed = jnp.concatenate([att * jax.nn.silu(gate_a), ssm * jax.nn.silu(gate_s)], axis=-1)
    y = jnp.einsum('bse,ed->bsd', mixed, w_out)
    return layer_norm(DEEPNORM_ALPHA * x + y, ln_g, ln_b)


def setup_inputs(seed: int = 0) -> dict:
    key = jax.random.key(seed)
    ks = jax.random.split(key, 19)
    f = jnp.float32
    L = DEPTH
    nrm = jax.random.normal
    x = nrm(ks[0], (BATCH, SEQ, D_MODEL), f)
    w_in = nrm(ks[1], (L, D_MODEL, D_IN), f) * (D_MODEL ** -0.5)
    kv_norm_g = 1.0 + 0.1 * nrm(ks[2], (L, KV_LATENT), f)
    w_uk = nrm(ks[3], (L, ATT_HEADS, KV_LATENT, ATT_HEAD_DIM), f) * (KV_LATENT ** -0.5)
    w_uv = nrm(ks[4], (L, ATT_HEADS, KV_LATENT, ATT_HEAD_DIM), f) * (KV_LATENT ** -0.5) * DEEPNORM_BETA
    log_dt = jax.random.uniform(ks[5], (L, N_GROUPS), f, minval=math.log(DT_MIN), maxval=math.log(DT_MAX))
    n = jnp.arange(STATE, dtype=f)
    a_re = -0.5 + 0.01 * nrm(ks[6], (L, N_GROUPS, STATE), f)
    a_im = jnp.pi * n + 0.01 * nrm(ks[7], (L, N_GROUPS, STATE), f)
    b_re = nrm(ks[8], (L, N_GROUPS, STATE, GROUP_CH), f) * ((2 * GROUP_CH) ** -0.5)
    b_im = nrm(ks[9], (L, N_GROUPS, STATE, GROUP_CH), f) * ((2 * GROUP_CH) ** -0.5)
    c_re = nrm(ks[10], (L, N_GROUPS, GROUP_CH, STATE), f) * ((2 * STATE) ** -0.5)
    c_im = nrm(ks[11], (L, N_GROUPS, GROUP_CH, STATE), f) * ((2 * STATE) ** -0.5)
    d_skip = nrm(ks[12], (L, N_GROUPS, GROUP_CH), f)
    w_glu = nrm(ks[13], (L, SSM_WIDTH, 2 * SSM_WIDTH), f) * (SSM_WIDTH ** -0.5)
    b_glu = 0.01 * nrm(ks[14], (L, 2 * SSM_WIDTH), f)
    w_out = nrm(ks[15], (L, D_MIX, D_MODEL), f) * (D_MIX ** -0.5) * DEEPNORM_BETA
    ln_g = 1.0 + 0.1 * nrm(ks[16], (L, D_MODEL), f)
    ln_b = 0.01 * nrm(ks[17], (L, D_MODEL), f)
    return {'x': x, 'w_in': w_in, 'kv_norm_g': kv_norm_g, 'w_uk': w_uk, 'w_uv': w_uv,
            'log_dt': log_dt, 'a_re': a_re, 'a_im': a_im, 'b_re': b_re, 'b_im': b_im,
            'c_re': c_re, 'c_im': c_im, 'd_skip': d_skip, 'w_glu': w_glu, 'b_glu': b_glu,
            'w_out': w_out, 'ln_g': ln_g, 'ln_b': ln_b}


def reference(x, w_in, kv_norm_g, w_uk, w_uv, log_dt, a_re, a_im, b_re, b_im, c_re, c_im, d_skip, w_glu, b_glu, w_out, ln_g, ln_b):
    h = x
    for l in range(DEPTH):
        h = hybrid_layer(h, w_in[l], kv_norm_g[l], w_uk[l], w_uv[l], log_dt[l], a_re[l], a_im[l],
                         b_re[l], b_im[l], c_re[l], c_im[l], d_skip[l], w_glu[l], b_glu[l],
                         w_out[l], ln_g[l], ln_b[l])
    return h
```

```python
import functools
import math

import jax
import jax.numpy as jnp
from jax import lax
from jax.experimental import pallas as pl
from jax.experimental.pallas import tpu as pltpu

ATT_HEADS = 8
ATT_HEAD_DIM = 64
ATT_WIDTH = ATT_HEADS * ATT_HEAD_DIM
KV_LATENT = 128
IDX_HEADS = 8
IDX_DIM = 32
TOPK_MAX = 256
SSM_WIDTH = 512
GROUP_CH = 16
N_GROUPS = SSM_WIDTH // GROUP_CH
STATE = 64
N_STATES = N_GROUPS * STATE
LN_EPS = 1e-5
RMS_EPS = 1e-6

LANES = 128
SUBLANES = 8
VMEM_LIMIT_BYTES = 56 * 1024 * 1024

Q_TILE = 128
KEY_TILE = 128
KV_TILE = 2 * KEY_TILE
INT_MIN = -(2 ** 31)
MASK_BIAS = -1e30
LOG2E = 1.4426950408889634

F32 = jnp.float32
BF16 = jnp.bfloat16
I32 = jnp.int32

_NT = (((1,), (1,)), ((), ()))


def _dot(a, b):
    return jnp.dot(a, b, preferred_element_type=F32)


def _dot_nt(a, b):
    return lax.dot_general(a, b, _NT, preferred_element_type=F32)


def _sigmoid(x):
    return 1.0 / (1.0 + jnp.exp(-x))


def _silu(x):
    return x * _sigmoid(x)


def _gelu_tanh(x):
    return 0.5 * x * (1.0 + jnp.tanh(math.sqrt(2.0 / math.pi) * (x + 0.044715 * (x * x * x))))


def _inproj_kernel(x_ref, wq_ref, wsm_ref, wqi_ref, wga_ref, wu_ref, wgs_ref, wukt_ref, kvg_ref,
                   qlat_ref, qi_ref, ki_ref, wit_ref, c_ref, ga_ref, u_ref, gs_ref):
    x = x_ref[0].astype(BF16)
    q = _dot(x, wq_ref[...])
    q_scale = (ATT_HEAD_DIM ** -0.5) * LOG2E
    for h in range(ATT_HEADS):
        qh = q[:, h * ATT_HEAD_DIM:(h + 1) * ATT_HEAD_DIM].astype(BF16)
        qlat_ref[0, h] = (_dot(qh, wukt_ref[h]) * q_scale).astype(BF16)
    sm = _dot(x, wsm_ref[...])
    ckv = sm[:, :KV_LATENT]
    c = ckv * lax.rsqrt(jnp.mean(ckv * ckv, axis=-1, keepdims=True) + RMS_EPS) * kvg_ref[...]
    c_ref[0] = c.astype(BF16)
    ki_ref[0] = sm[:, KV_LATENT:KV_LATENT + IDX_DIM].astype(BF16)
    sm_t = sm[:, KV_LATENT:].T
    wit_ref[0] = sm_t[IDX_DIM:IDX_DIM + IDX_HEADS, :] * (IDX_HEADS ** -0.5)
    qi = _dot(x, wqi_ref[...]) * (IDX_DIM ** -0.5)
    for h in range(IDX_HEADS):
        qi_ref[0, h] = qi[:, h * IDX_DIM:(h + 1) * IDX_DIM].astype(BF16)
    ga_ref[0] = _dot(x, wga_ref[...])
    u_ref[...] = _dot(x, wu_ref[...])
    gs_ref[...] = _dot(x, wgs_ref[...])


def _in_projection(x, wq, wsm, wqi, wga, wu, wgs, wukt, kvg, tile):
    bsz, seq, d = x.shape
    nt = seq // tile
    const2 = lambda b, i: (0, 0)
    const3 = lambda b, i: (0, 0, 0)
    out_shape = (
        jax.ShapeDtypeStruct((bsz, ATT_HEADS, seq, KV_LATENT), BF16),
        jax.ShapeDtypeStruct((bsz, IDX_HEADS, seq, IDX_DIM), BF16),
        jax.ShapeDtypeStruct((bsz, seq, IDX_DIM), BF16),
        jax.ShapeDtypeStruct((bsz, IDX_HEADS, seq), F32),
        jax.ShapeDtypeStruct((bsz, seq, KV_LATENT), BF16),
        jax.ShapeDtypeStruct((bsz, seq, ATT_WIDTH), F32),
        jax.ShapeDtypeStruct((seq, bsz * SSM_WIDTH), F32),
        jax.ShapeDtypeStruct((seq, bsz * SSM_WIDTH), F32),
    )
    out_specs = (
        pl.BlockSpec((1, ATT_HEADS, tile, KV_LATENT), lambda b, i: (b, 0, i, 0)),
        pl.BlockSpec((1, IDX_HEADS, tile, IDX_DIM), lambda b, i: (b, 0, i, 0)),
        pl.BlockSpec((1, tile, IDX_DIM), lambda b, i: (b, i, 0)),
        pl.BlockSpec((1, IDX_HEADS, tile), lambda b, i: (b, 0, i)),
        pl.BlockSpec((1, tile, KV_LATENT), lambda b, i: (b, i, 0)),
        pl.BlockSpec((1, tile, ATT_WIDTH), lambda b, i: (b, i, 0)),
        pl.BlockSpec((tile, SSM_WIDTH), lambda b, i: (i, b)),
        pl.BlockSpec((tile, SSM_WIDTH), lambda b, i: (i, b)),
    )
    in_specs = [
        pl.BlockSpec((1, tile, d), lambda b, i: (b, i, 0)),
        pl.BlockSpec(wq.shape, const2),
        pl.BlockSpec(wsm.shape, const2),
        pl.BlockSpec(wqi.shape, const2),
        pl.BlockSpec(wga.shape, const2),
        pl.BlockSpec(wu.shape, const2),
        pl.BlockSpec(wgs.shape, const2),
        pl.BlockSpec(wukt.shape, const3),
        pl.BlockSpec(kvg.shape, const2),
    ]
    return pl.pallas_call(
        _inproj_kernel,
        grid=(bsz, nt),
        in_specs=in_specs,
        out_specs=out_specs,
        out_shape=out_shape,
        compiler_params=pltpu.CompilerParams(
            dimension_semantics=("arbitrary", "arbitrary"), vmem_limit_bytes=VMEM_LIMIT_BYTES),
        name="in_projection",
    )(x, wq, wsm, wqi, wga, wu, wgs, wukt, kvg)


def _attention_kernel(qi_ref, wit_ref, ki_ref, ql_ref, c_ref, wuv_ref, out_ref,
                      keys_scr, bias_scr, caug_scr, m_scr, acc_scr, *, n_sel):
    j = pl.program_id(1)
    n_key_tiles = j + 1
    n_kv_tiles = (j + 2) // 2
    odd_tail = (n_key_tiles % 2) == 1

    @pl.when(j == 0)
    def _():
        caug_scr[:, :KV_LATENT] = c_ref[0]
        caug_scr[:, KV_LATENT:] = jnp.ones((caug_scr.shape[0], KV_LATENT), BF16)

    q_pos = j * Q_TILE + lax.broadcasted_iota(I32, (KEY_TILE, Q_TILE), 1)
    k_off = lax.broadcasted_iota(I32, (KEY_TILE, Q_TILE), 0)
    qi = qi_ref[0].reshape(IDX_HEADS * Q_TILE, IDX_DIM)

    def score_tile(t, carry):
        r0 = pl.multiple_of(t * KEY_TILE, KEY_TILE)
        logits = _dot_nt(ki_ref[0, pl.ds(r0, KEY_TILE), :], qi)
        score = jnp.zeros((KEY_TILE, Q_TILE), F32)
        for h in range(IDX_HEADS):
            score = score + wit_ref[0, h:h + 1, :] * jnp.maximum(logits[:, h * Q_TILE:(h + 1) * Q_TILE], 0.0)
        bits = pltpu.bitcast(score, I32)
        key = jnp.where(bits < 0, bits ^ jnp.int32(0x7FFFFFFF), bits)
        key = jnp.where(score == 0.0, 0, key)
        key = jnp.where(t * KEY_TILE + k_off <= q_pos, key, INT_MIN)
        keys_scr[pl.ds(r0, KEY_TILE), :] = key
        return carry

    lax.fori_loop(0, n_key_tiles, score_tile, 0)

    @pl.when(odd_tail)
    def _():
        r0 = pl.multiple_of(n_key_tiles * KEY_TILE, KEY_TILE)
        keys_scr[pl.ds(r0, KEY_TILE), :] = jnp.full((KEY_TILE, Q_TILE), INT_MIN, I32)
        bias_scr[n_key_tiles] = jnp.full((Q_TILE, KEY_TILE), MASK_BIAS, F32)

    def bit_pass(i, tau):
        cand = tau + jnp.left_shift(jnp.int32(1), 31 - i)

        def count_tile(t, acc):
            r0 = pl.multiple_of(t * KV_TILE, KV_TILE)
            ge = jnp.where(keys_scr[pl.ds(r0, KV_TILE), :] >= cand, 1, 0).astype(I32)
            return acc + jnp.sum(ge.reshape(KV_TILE // SUBLANES, SUBLANES, Q_TILE), axis=0)

        acc = lax.fori_loop(0, n_kv_tiles, count_tile, jnp.zeros((SUBLANES, Q_TILE), I32))
        count = jnp.sum(acc, axis=0, keepdims=True)
        return jnp.where(count >= n_sel, cand, tau)

    tau = lax.fori_loop(0, 32, bit_pass, jnp.full((1, Q_TILE), INT_MIN, I32))

    def bias_tile(t, carry):
        r0 = pl.multiple_of(t * KEY_TILE, KEY_TILE)
        sel = (keys_scr[pl.ds(r0, KEY_TILE), :] >= tau) & (t * KEY_TILE + k_off <= q_pos)
        bias_scr[t] = jnp.where(sel, 0.0, MASK_BIAS).astype(F32).T
        return carry

    lax.fori_loop(0, n_key_tiles, bias_tile, 0)

    q = ql_ref[0].reshape(ATT_HEADS * Q_TILE, KV_LATENT)
    m_scr[...] = jnp.full(m_scr.shape, MASK_BIAS, F32)
    acc_scr[...] = jnp.zeros(acc_scr.shape, F32)

    def kv_tile(t, carry):
        r0 = pl.multiple_of(t * KV_TILE, KV_TILE)
        s = _dot_nt(q, c_ref[0, pl.ds(r0, KV_TILE), :])
        c_aug = caug_scr[pl.ds(r0, KV_TILE), :]
        b0 = bias_scr[2 * t]
        b1 = bias_scr[2 * t + 1]
        for h in range(ATT_HEADS):
            rows = slice(h * Q_TILE, (h + 1) * Q_TILE)
            s0 = s[rows, :KEY_TILE] + b0
            s1 = s[rows, KEY_TILE:] + b1
            m_prev = m_scr[rows, :]
            m_new = jnp.maximum(m_prev, jnp.max(jnp.maximum(s0, s1), axis=-1, keepdims=True))
            alpha = jnp.exp2(m_prev - m_new)
            p = jnp.concatenate([jnp.exp2(s0 - m_new), jnp.exp2(s1 - m_new)], axis=1).astype(BF16)
            acc_scr[rows, :] = jnp.concatenate([alpha, alpha], axis=1) * acc_scr[rows, :] + _dot(p, c_aug)
            m_scr[rows, :] = m_new
        return carry

    lax.fori_loop(0, n_kv_tiles, kv_tile, 0)

    for h in range(ATT_HEADS):
        a = acc_scr[h * Q_TILE:(h + 1) * Q_TILE, :]
        o_lat = a[:, :KV_LATENT] / a[:, KV_LATENT:]
        out_ref[0, :, h * ATT_HEAD_DIM:(h + 1) * ATT_HEAD_DIM] = _dot(o_lat.astype(BF16), wuv_ref[h])


def _sparse_attention(qlat, qi, ki, wit, c, wuv):
    bsz, _, seq, _ = qlat.shape
    nq = seq // Q_TILE
    n_sel = min(TOPK_MAX, seq // 4)
    seq_pad = ((seq + KV_TILE - 1) // KV_TILE) * KV_TILE
    kernel = functools.partial(_attention_kernel, n_sel=n_sel)
    return pl.pallas_call(
        kernel,
        grid=(bsz, nq),
        in_specs=[
            pl.BlockSpec((1, IDX_HEADS, Q_TILE, IDX_DIM), lambda b, j: (b, 0, j, 0)),
            pl.BlockSpec((1, IDX_HEADS, Q_TILE), lambda b, j: (b, 0, j)),
            pl.BlockSpec((1, seq, IDX_DIM), lambda b, j: (b, 0, 0)),
            pl.BlockSpec((1, ATT_HEADS, Q_TILE, KV_LATENT), lambda b, j: (b, 0, j, 0)),
            pl.BlockSpec((1, seq, KV_LATENT), lambda b, j: (b, 0, 0)),
            pl.BlockSpec(wuv.shape, lambda b, j: (0, 0, 0)),
        ],
        out_specs=pl.BlockSpec((1, Q_TILE, ATT_WIDTH), lambda b, j: (b, j, 0)),
        out_shape=jax.ShapeDtypeStruct((bsz, seq, ATT_WIDTH), F32),
        scratch_shapes=[
            pltpu.VMEM((seq_pad, Q_TILE), I32),
            pltpu.VMEM((seq_pad // KEY_TILE, Q_TILE, KEY_TILE), F32),
            pltpu.VMEM((seq_pad, 2 * KV_LATENT), BF16),
            pltpu.VMEM((ATT_HEADS * Q_TILE, LANES), F32),
            pltpu.VMEM((ATT_HEADS * Q_TILE, 2 * KV_LATENT), F32),
        ],
        compiler_params=pltpu.CompilerParams(
            dimension_semantics=("arbitrary", "arbitrary"), vmem_limit_bytes=VMEM_LIMIT_BYTES),
        name="sparse_attention",
    )(qi, wit, ki, qlat, c, wuv)


HALF_STATES = N_STATES // 2
HALF_COLS = 2 * HALF_STATES
SCAN_COLS = 256


def _s5_kernel(u_ref, gs_ref, bmat_ref, a_ref, cmat_ref, dskip_ref, wglu_ref, bglu_ref, out_ref,
               x_scr, state_scr):
    tile, bsz, _ = u_ref.shape
    rows = tile * bsz

    @pl.when(pl.program_id(0) == 0)
    def _():
        state_scr[...] = jnp.zeros(state_scr.shape, F32)

    u = u_ref[...].reshape(rows, SSM_WIDTH)
    ub = u.astype(BF16)
    half_in = SSM_WIDTH // 2
    x_scr[:, :HALF_COLS] = _dot(ub[:, :half_in], bmat_ref[0])
    x_scr[:, HALF_COLS:] = _dot(ub[:, half_in:], bmat_ref[1])

    for half in range(2):
        for off in range(0, HALF_STATES, SCAN_COLS):
            n0 = half * HALF_STATES + off
            cr = half * HALF_COLS + off
            ci = cr + HALF_STATES
            a_re = jnp.broadcast_to(a_ref[0:1, n0:n0 + SCAN_COLS], (bsz, SCAN_COLS))
            a_im = jnp.broadcast_to(a_ref[1:2, n0:n0 + SCAN_COLS], (bsz, SCAN_COLS))

            def step(t, carry, cr=cr, ci=ci, a_re=a_re, a_im=a_im):
                x_re, x_im = carry
                r0 = pl.multiple_of(t * bsz, bsz)
                n_re = a_re * x_re - a_im * x_im + x_scr[pl.ds(r0, bsz), cr:cr + SCAN_COLS]
                n_im = a_re * x_im + a_im * x_re + x_scr[pl.ds(r0, bsz), ci:ci + SCAN_COLS]
                x_scr[pl.ds(r0, bsz), cr:cr + SCAN_COLS] = n_re
                x_scr[pl.ds(r0, bsz), ci:ci + SCAN_COLS] = n_im
                return n_re, n_im

            x_re, x_im = lax.fori_loop(
                0, tile, step,
                (state_scr[:, cr:cr + SCAN_COLS], state_scr[:, ci:ci + SCAN_COLS]), unroll=4)
            state_scr[:, cr:cr + SCAN_COLS] = x_re
            state_scr[:, ci:ci + SCAN_COLS] = x_im

    y0 = _dot(x_scr[:, :HALF_COLS].astype(BF16), cmat_ref[0])
    y1 = _dot(x_scr[:, HALF_COLS:].astype(BF16), cmat_ref[1])
    y = jnp.concatenate([y0, y1], axis=1) + dskip_ref[...] * u
    y = _gelu_tanh(y)
    hcat = _dot(y.astype(BF16), wglu_ref[...]) + bglu_ref[...]
    ssm = hcat[:, :SSM_WIDTH] * _sigmoid(hcat[:, SSM_WIDTH:])
    gs = gs_ref[...].reshape(rows, SSM_WIDTH)
    out_ref[...] = (ssm * _silu(gs)).reshape(tile, bsz, SSM_WIDTH)


def _s5_branch(u_t, gs_t, bmat, a_bar, cmat, dskip, wglu, bglu, tile):
    seq, bsz, _ = u_t.shape
    const2 = lambda i: (0, 0)
    const3 = lambda i: (0, 0, 0)
    return pl.pallas_call(
        _s5_kernel,
        grid=(seq // tile,),
        in_specs=[
            pl.BlockSpec((tile, bsz, SSM_WIDTH), lambda i: (i, 0, 0)),
            pl.BlockSpec((tile, bsz, SSM_WIDTH), lambda i: (i, 0, 0)),
            pl.BlockSpec(bmat.shape, const3),
            pl.BlockSpec(a_bar.shape, const2),
            pl.BlockSpec(cmat.shape, const3),
            pl.BlockSpec(dskip.shape, const2),
            pl.BlockSpec(wglu.shape, const2),
            pl.BlockSpec(bglu.shape, const2),
        ],
        out_specs=pl.BlockSpec((tile, bsz, SSM_WIDTH), lambda i: (i, 0, 0)),
        out_shape=jax.ShapeDtypeStruct((seq, bsz, SSM_WIDTH), F32),
        scratch_shapes=[
            pltpu.VMEM((tile * bsz, 2 * N_STATES), F32),
            pltpu.VMEM((bsz, 2 * N_STATES), F32),
        ],
        compiler_params=pltpu.CompilerParams(
            dimension_semantics=("arbitrary",), vmem_limit_bytes=VMEM_LIMIT_BYTES),
        name="s5_branch",
    )(u_t, gs_t, bmat, a_bar, cmat, dskip, wglu, bglu)


def _s5_matrices(log_dt, a_re, a_im, b_re, b_im, c_re, c_im):
    dt = jnp.exp(log_dt.astype(F32))[:, None]
    lam_re, lam_im = a_re.astype(F32), a_im.astype(F32)
    mag = jnp.exp(lam_re * dt)
    ab_re = mag * jnp.cos(lam_im * dt)
    ab_im = mag * jnp.sin(lam_im * dt)
    den = lam_re * lam_re + lam_im * lam_im
    k_re = ((ab_re - 1.0) * lam_re + ab_im * lam_im) / den
    k_im = (ab_im * lam_re - (ab_re - 1.0) * lam_im) / den
    bb_re = k_re[..., None] * b_re - k_im[..., None] * b_im
    bb_im = k_re[..., None] * b_im + k_im[..., None] * b_re
    gh = N_GROUPS // 2
    eye = jnp.eye(gh, dtype=F32)

    def in_block(bb):
        return jnp.einsum('gpc,gh->gchp', bb, eye).reshape(gh * GROUP_CH, gh * STATE)

    def out_block(cc):
        return jnp.einsum('gcp,gh->gphc', cc, eye).reshape(gh * STATE, gh * GROUP_CH)

    bmat = jnp.stack([
        jnp.concatenate([in_block(bb_re[k * gh:(k + 1) * gh]), in_block(bb_im[k * gh:(k + 1) * gh])], axis=1)
        for k in range(2)])
    cmat = jnp.stack([
        jnp.concatenate([out_block(c_re[k * gh:(k + 1) * gh]), out_block(-c_im[k * gh:(k + 1) * gh])], axis=0)
        for k in range(2)])
    a_bar = jnp.stack([ab_re.reshape(N_STATES), ab_im.reshape(N_STATES)])
    return bmat.astype(BF16), a_bar, cmat.astype(BF16)


def _outproj_kernel(att_ref, ga_ref, ssm_ref, x_ref, wo_ref, lng_ref, lnb_ref, out_ref, *, alpha):
    att = att_ref[0] * _silu(ga_ref[0])
    y = _dot(att.astype(BF16), wo_ref[:ATT_WIDTH, :]) + _dot(ssm_ref[...].astype(BF16), wo_ref[ATT_WIDTH:, :])
    z = alpha * x_ref[0] + y
    mu = jnp.mean(z, axis=-1, keepdims=True)
    zc = z - mu
    var = jnp.mean(zc * zc, axis=-1, keepdims=True)
    out_ref[0] = zc * lax.rsqrt(var + LN_EPS) * lng_ref[...] + lnb_ref[...]


def _out_projection(att, ga, ssm_t, x, wo, lng, lnb, alpha, tile):
    bsz, seq, d = x.shape
    const2 = lambda b, i: (0, 0)
    return pl.pallas_call(
        functools.partial(_outproj_kernel, alpha=alpha),
        grid=(bsz, seq // tile),
        in_specs=[
            pl.BlockSpec((1, tile, ATT_WIDTH), lambda b, i: (b, i, 0)),
            pl.BlockSpec((1, tile, ATT_WIDTH), lambda b, i: (b, i, 0)),
            pl.BlockSpec((tile, SSM_WIDTH), lambda b, i: (i, b)),
            pl.BlockSpec((1, tile, d), lambda b, i: (b, i, 0)),
            pl.BlockSpec(wo.shape, const2),
            pl.BlockSpec(lng.shape, const2),
            pl.BlockSpec(lnb.shape, const2),
        ],
        out_specs=pl.BlockSpec((1, tile, d), lambda b, i: (b, i, 0)),
        out_shape=jax.ShapeDtypeStruct((bsz, seq, d), F32),
        compiler_params=pltpu.CompilerParams(
            dimension_semantics=("arbitrary", "arbitrary"), vmem_limit_bytes=VMEM_LIMIT_BYTES),
        name="out_projection",
    )(att, ga, ssm_t, x, wo, lng, lnb)


def _split_w_in(w_in):
    sizes = (ATT_WIDTH, KV_LATENT, IDX_HEADS * IDX_DIM, IDX_DIM, IDX_HEADS, ATT_WIDTH, SSM_WIDTH, SSM_WIDTH)
    parts, acc = [], 0
    for s in sizes:
        parts.append(w_in[:, acc:acc + s])
        acc += s
    wq, wckv, wqi, wki, wwi, wga, wu, wgs = parts
    pad = jnp.zeros((w_in.shape[0], 2 * LANES - KV_LATENT - IDX_DIM - IDX_HEADS), w_in.dtype)
    wsm = jnp.concatenate([wckv, wki, wwi, pad], axis=1)
    return [w.astype(BF16) for w in (wq, wsm, wqi, wga, wu, wgs)]


def _layer(x, alpha, w_in, kv_g, w_uk, w_uv, log_dt, a_re, a_im, b_re, b_im, c_re, c_im, d_skip,
           w_glu, b_glu, w_out, ln_g, ln_b):
    bsz, seq, d = x.shape
    row_tile = min(512, seq)
    scan_tile = min(32, seq)
    wq, wsm, wqi, wga, wu, wgs = _split_w_in(w_in)
    wukt = jnp.swapaxes(w_uk, 1, 2).astype(BF16)
    qlat, qi, ki, wit, c, ga, u_t, gs_t = _in_projection(
        x, wq, wsm, wqi, wga, wu, wgs, wukt, kv_g.reshape(1, KV_LATENT).astype(F32), row_tile)
    att = _sparse_attention(qlat, qi, ki, wit, c, w_uv.astype(BF16))
    bmat, a_bar, cmat = _s5_matrices(log_dt, a_re, a_im, b_re, b_im, c_re, c_im)
    ssm_t = _s5_branch(
        u_t.reshape(seq, bsz, SSM_WIDTH), gs_t.reshape(seq, bsz, SSM_WIDTH), bmat, a_bar, cmat,
        d_skip.reshape(1, SSM_WIDTH).astype(F32), w_glu.astype(BF16),
        b_glu.reshape(1, 2 * SSM_WIDTH).astype(F32), scan_tile)
    return _out_projection(
        att, ga, ssm_t.reshape(seq, bsz * SSM_WIDTH), x, w_out.astype(BF16),
        ln_g.reshape(1, d).astype(F32), ln_b.reshape(1, d).astype(F32), alpha, row_tile)


def kernel(x, w_in, kv_norm_g, w_uk, w_uv, log_dt, a_re, a_im, b_re, b_im, c_re, c_im, d_skip, w_glu, b_glu, w_out, ln_g, ln_b):
    depth = w_in.shape[0]
    alpha = (2 * depth) ** 0.25
    h = x
    for l in range(depth):
        h = _layer(h, alpha, w_in[l], kv_norm_g[l], w_uk[l], w_uv[l], log_dt[l], a_re[l], a_im[l],
                   b_re[l], b_im[l], c_re[l], c_im[l], d_skip[l], w_glu[l], b_glu[l],
                   w_out[l], ln_g[l], ln_b[l])
    return h
```

```python
import functools
import math

import jax
import jax.numpy as jnp
from jax import lax
from jax.experimental import pallas as pl
from jax.experimental.pallas import tpu as pltpu

ATT_HEADS = 8
ATT_HEAD_DIM = 64
ATT_WIDTH = ATT_HEADS * ATT_HEAD_DIM
KV_LATENT = 128
IDX_HEADS = 8
IDX_DIM = 32
TOPK_MAX = 256
SSM_WIDTH = 512
GROUP_CH = 16
N_GROUPS = SSM_WIDTH // GROUP_CH
STATE = 64
N_STATES = N_GROUPS * STATE
LN_EPS = 1e-5
RMS_EPS = 1e-6

LANES = 128
SUBLANES = 8
VMEM_LIMIT_BYTES = 56 * 1024 * 1024

Q_TILE = 128
KEY_TILE = 128
KV_TILE = 2 * KEY_TILE
INT_MIN = -(2 ** 31)
MASK_BIAS = -1e30
LOG2E = 1.4426950408889634

F32 = jnp.float32
BF16 = jnp.bfloat16
I32 = jnp.int32

_NT = (((1,), (1,)), ((), ()))


def _dot(a, b):
    return jnp.dot(a, b, preferred_element_type=F32)


def _dot_nt(a, b):
    return lax.dot_general(a, b, _NT, preferred_element_type=F32)


def _sigmoid(x):
    return 1.0 / (1.0 + jnp.exp(-x))


def _silu(x):
    return x * _sigmoid(x)


def _gelu_tanh(x):
    return 0.5 * x * (1.0 + jnp.tanh(math.sqrt(2.0 / math.pi) * (x + 0.044715 * (x * x * x))))


def _inproj_kernel(x_ref, wq_ref, wsm_ref, wqi_ref, wga_ref, wu_ref, wgs_ref, wukt_ref, kvg_ref,
                   qlt_ref, qi_ref, ki_ref, wit_ref, c_ref, ct_ref, ga_ref, u_ref, gs_ref):
    x = x_ref[0].astype(BF16)
    q = _dot(x, wq_ref[...])
    q_scale = (ATT_HEAD_DIM ** -0.5) * LOG2E
    for h in range(ATT_HEADS):
        qh = q[:, h * ATT_HEAD_DIM:(h + 1) * ATT_HEAD_DIM].astype(BF16)
        ql_t = (_dot(qh, wukt_ref[h]) * q_scale).T.astype(BF16)
        for g in range(x.shape[0] // Q_TILE):
            qlt_ref[0, g, :, h * Q_TILE:(h + 1) * Q_TILE] = ql_t[:, g * Q_TILE:(g + 1) * Q_TILE]
    sm = _dot(x, wsm_ref[...])
    ckv = sm[:, :KV_LATENT]
    c = ckv * lax.rsqrt(jnp.mean(ckv * ckv, axis=-1, keepdims=True) + RMS_EPS) * kvg_ref[...]
    c_ref[0] = c.astype(BF16)
    c_t = c.T.astype(BF16)
    for g in range(x.shape[0] // KV_TILE):
        ct_ref[0, g] = c_t[:, g * KV_TILE:(g + 1) * KV_TILE]
    ki_ref[0] = sm[:, KV_LATENT:KV_LATENT + IDX_DIM].astype(BF16)
    sm_t = sm[:, KV_LATENT:].T
    wit_ref[0] = sm_t[IDX_DIM:IDX_DIM + IDX_HEADS, :] * (IDX_HEADS ** -0.5)
    qi = _dot(x, wqi_ref[...]) * (IDX_DIM ** -0.5)
    for h in range(IDX_HEADS):
        qi_ref[0, h] = qi[:, h * IDX_DIM:(h + 1) * IDX_DIM].astype(BF16)
    ga_ref[0] = _dot(x, wga_ref[...])
    u_ref[...] = _dot(x, wu_ref[...])
    gs_ref[...] = _dot(x, wgs_ref[...])


def _in_projection(x, wq, wsm, wqi, wga, wu, wgs, wukt, kvg, tile):
    bsz, seq, d = x.shape
    nt = seq // tile
    const2 = lambda b, i: (0, 0)
    const3 = lambda b, i: (0, 0, 0)
    out_shape = (
        jax.ShapeDtypeStruct((bsz, seq // Q_TILE, KV_LATENT, ATT_HEADS * Q_TILE), BF16),
        jax.ShapeDtypeStruct((bsz, IDX_HEADS, seq, IDX_DIM), BF16),
        jax.ShapeDtypeStruct((bsz, seq, IDX_DIM), BF16),
        jax.ShapeDtypeStruct((bsz, IDX_HEADS, seq), F32),
        jax.ShapeDtypeStruct((bsz, seq, KV_LATENT), BF16),
        jax.ShapeDtypeStruct((bsz, seq // KV_TILE, KV_LATENT, KV_TILE), BF16),
        jax.ShapeDtypeStruct((bsz, seq, ATT_WIDTH), F32),
        jax.ShapeDtypeStruct((seq, bsz * SSM_WIDTH), F32),
        jax.ShapeDtypeStruct((seq, bsz * SSM_WIDTH), F32),
    )
    out_specs = (
        pl.BlockSpec((1, tile // Q_TILE, KV_LATENT, ATT_HEADS * Q_TILE), lambda b, i: (b, i, 0, 0)),
        pl.BlockSpec((1, IDX_HEADS, tile, IDX_DIM), lambda b, i: (b, 0, i, 0)),
        pl.BlockSpec((1, tile, IDX_DIM), lambda b, i: (b, i, 0)),
        pl.BlockSpec((1, IDX_HEADS, tile), lambda b, i: (b, 0, i)),
        pl.BlockSpec((1, tile, KV_LATENT), lambda b, i: (b, i, 0)),
        pl.BlockSpec((1, tile // KV_TILE, KV_LATENT, KV_TILE), lambda b, i: (b, i, 0, 0)),
        pl.BlockSpec((1, tile, ATT_WIDTH), lambda b, i: (b, i, 0)),
        pl.BlockSpec((tile, SSM_WIDTH), lambda b, i: (i, b)),
        pl.BlockSpec((tile, SSM_WIDTH), lambda b, i: (i, b)),
    )
    in_specs = [
        pl.BlockSpec((1, tile, d), lambda b, i: (b, i, 0)),
        pl.BlockSpec(wq.shape, const2),
        pl.BlockSpec(wsm.shape, const2),
        pl.BlockSpec(wqi.shape, const2),
        pl.BlockSpec(wga.shape, const2),
        pl.BlockSpec(wu.shape, const2),
        pl.BlockSpec(wgs.shape, const2),
        pl.BlockSpec(wukt.shape, const3),
        pl.BlockSpec(kvg.shape, const2),
    ]
    return pl.pallas_call(
        _inproj_kernel,
        grid=(bsz, nt),
        in_specs=in_specs,
        out_specs=out_specs,
        out_shape=out_shape,
        compiler_params=pltpu.CompilerParams(
            dimension_semantics=("arbitrary", "arbitrary"), vmem_limit_bytes=VMEM_LIMIT_BYTES),
        name="in_projection",
    )(x, wq, wsm, wqi, wga, wu, wgs, wukt, kvg)


CNT_TILE = 512
PACK16 = 16
I16 = jnp.int16
I16_MIN = -(2 ** 15)
MAX_SEQ = 256 * PACK16


def _tree_sum(xs):
    xs = list(xs)
    while len(xs) > 1:
        nxt = [xs[i] + xs[i + 1] for i in range(0, len(xs) - 1, 2)]
        if len(xs) % 2:
            nxt.append(xs[-1])
        xs = nxt
    return xs[0]


def _radix_select16(src_scr, n_cnt_tiles, k_target, ge_all):
    groups = CNT_TILE // PACK16
    one, zero = jnp.ones((), BF16), jnp.zeros((), BF16)

    def bit_pass(i, state):
        tau, c_ge, c_gt = state
        cand = tau + jnp.left_shift(jnp.int32(1), 15 - i)
        cand16 = jnp.broadcast_to(cand, (PACK16, Q_TILE)).astype(I16)

        def count_tile(t, acc):
            r0 = pl.multiple_of(t * CNT_TILE, CNT_TILE)
            x = src_scr[pl.ds(r0, CNT_TILE), :].reshape(groups, PACK16, Q_TILE)
            ge = jnp.where(x >= cand16[None], one, zero)
            return acc + _tree_sum([ge[r] for r in range(groups)])

        acc = lax.fori_loop(0, n_cnt_tiles, count_tile, jnp.zeros((PACK16, Q_TILE), BF16))
        count = jnp.sum(acc.astype(F32), axis=0, keepdims=True).astype(I32)
        ok = count >= k_target
        return jnp.where(ok, cand, tau), jnp.where(ok, count, c_ge), jnp.where(ok, c_gt, count)

    init = (jnp.full((1, Q_TILE), I16_MIN, I32), ge_all, jnp.zeros((1, Q_TILE), I32))
    return lax.fori_loop(0, 16, bit_pass, init)


def _attention_kernel(qi_ref, wit_ref, ki_ref, qlt_ref, c_ref, ct_ref, wuvt_ref, out_ref,
                      key_scr, khi_scr, klo_scr, ksel_scr, bias_scr,
                      sa_scr, sb_scr, pa_scr, pb_scr, aa_scr, ab_scr, m_scr, acc_scr, *, n_sel, idx_bits):
    j = pl.program_id(1)
    n_key_tiles = j + 1
    n_cnt_tiles = (j + 4) // 4
    n_pairs = n_cnt_tiles

    q_pos = j * Q_TILE + lax.broadcasted_iota(I32, (KEY_TILE, Q_TILE), 1)
    k_off = lax.broadcasted_iota(I32, (KEY_TILE, Q_TILE), 0)

    def store_keys(r0, key):
        key_scr[pl.ds(r0, KEY_TILE), :] = key
        khi_scr[pl.ds(r0, KEY_TILE), :] = jnp.right_shift(key, 16).astype(I16)
        klo_scr[pl.ds(r0, KEY_TILE), :] = ((key & 0xFFFF) + I16_MIN).astype(I16)

    def score_tiles(i, carry):
        for sub in range(CNT_TILE // KEY_TILE):
            t = i * (CNT_TILE // KEY_TILE) + sub
            r0 = pl.multiple_of(t * KEY_TILE, KEY_TILE)
            ki_t = ki_ref[0, pl.ds(r0, KEY_TILE), :]
            score = jnp.zeros((KEY_TILE, Q_TILE), F32)
            for h in range(IDX_HEADS):
                logits = _dot_nt(ki_t, qi_ref[0, h])
                score = score + wit_ref[0, h:h + 1, :] * jnp.maximum(logits, 0.0)
            bits = pltpu.bitcast(score, I32)
            key = jnp.where(bits < 0, bits ^ jnp.int32(0x7FFFFFFF), bits)
            key = jnp.where(score == 0.0, 0, key)
            store_keys(r0, jnp.where(t * KEY_TILE + k_off <= q_pos, key, INT_MIN))
        return carry

    lax.fori_loop(0, n_cnt_tiles, score_tiles, 0)

    k_sel = jnp.full((1, Q_TILE), n_sel, I32)
    n_rows = jnp.full((1, Q_TILE), n_cnt_tiles * CNT_TILE, I32)
    tau_hi, ge_hi, gt_hi = _radix_select16(khi_scr, n_cnt_tiles, k_sel, n_rows)
    tau_hi16 = jnp.broadcast_to(tau_hi, (PACK16, Q_TILE)).astype(I16)

    def sel_tile(t, carry):
        r0 = pl.multiple_of(t * CNT_TILE, CNT_TILE)
        hi = khi_scr[pl.ds(r0, CNT_TILE), :].reshape(CNT_TILE // PACK16, PACK16, Q_TILE)
        lo = klo_scr[pl.ds(r0, CNT_TILE), :].reshape(CNT_TILE // PACK16, PACK16, Q_TILE)
        sel = jnp.where(hi == tau_hi16[None], lo, jnp.full((), I16_MIN, I16))
        ksel_scr[pl.ds(r0, CNT_TILE), :] = sel.reshape(CNT_TILE, Q_TILE)
        return carry

    lax.fori_loop(0, n_cnt_tiles, sel_tile, 0)
    tau_lo, ge_lo, gt_lo = _radix_select16(ksel_scr, n_cnt_tiles, k_sel - gt_hi, ge_hi - gt_hi)
    tau = jnp.left_shift(tau_hi, 16) | (tau_lo - I16_MIN)
    need = n_sel - (gt_hi + gt_lo)
    excess = ((ge_lo - gt_lo) > need) & (tau != INT_MIN)

    @pl.when(jnp.max(jnp.where(excess, 1, 0)) > 0)
    def _():
        def idx_pass(i, lo_idx):
            cand = lo_idx + jnp.left_shift(jnp.int32(1), idx_bits - 1 - i)

            def count_tile(t, acc):
                r0 = pl.multiple_of(t * KEY_TILE, KEY_TILE)
                eq = jnp.where(key_scr[pl.ds(r0, KEY_TILE), :] == tau,
                               jnp.where(t * KEY_TILE + k_off < cand, 1, 0), 0)
                return acc + jnp.sum(eq.reshape(KEY_TILE // SUBLANES, SUBLANES, Q_TILE), axis=0)

            acc = lax.fori_loop(0, n_key_tiles, count_tile, jnp.zeros((SUBLANES, Q_TILE), I32))
            return jnp.where(jnp.sum(acc, axis=0, keepdims=True) < need, cand, lo_idx)

        cut = lax.fori_loop(0, idx_bits, idx_pass, jnp.zeros((1, Q_TILE), I32))

        def drop_tile(t, carry):
            r0 = pl.multiple_of(t * KEY_TILE, KEY_TILE)
            key = key_scr[pl.ds(r0, KEY_TILE), :]
            late = jnp.where(key == tau, jnp.where(t * KEY_TILE + k_off > cut, 1, 0), 0)
            key_scr[pl.ds(r0, KEY_TILE), :] = jnp.where(late == 1, INT_MIN, key)
            return carry

        lax.fori_loop(0, n_key_tiles, drop_tile, 0)

    ones_rows = jnp.ones((PACK16, KV_TILE), BF16)

    def scores(t, s_scr):
        r0 = pl.multiple_of(t * KV_TILE, KV_TILE)
        s_scr[...] = _dot(c_ref[0, pl.ds(r0, KV_TILE), :], qlt_ref[0, 0])

    def softmax(t, s_scr, p_scr, a_scr):
        for half in range(KV_TILE // KEY_TILE):
            kt = 2 * t + half
            r0 = pl.multiple_of(kt * KEY_TILE, KEY_TILE)
            bias_scr[half * KEY_TILE:(half + 1) * KEY_TILE, :] = jnp.where(
                key_scr[pl.ds(r0, KEY_TILE), :] >= tau,
                jnp.where(kt * KEY_TILE + k_off <= q_pos, 0.0, MASK_BIAS), MASK_BIAS).astype(F32)
        for h in range(ATT_HEADS):
            cols = slice(h * Q_TILE, (h + 1) * Q_TILE)
            s = s_scr[:, cols] + bias_scr[...]
            m_prev = m_scr[:, cols]
            m_new = jnp.maximum(m_prev, jnp.max(s, axis=0, keepdims=True))
            a_scr[:, cols] = jnp.exp2(m_prev - m_new)
            p_scr[:, cols] = jnp.exp2(s - m_new).astype(BF16)
            m_scr[:, cols] = m_new

    def values(t, p_scr, a_scr):
        c_aug_t = jnp.concatenate([ct_ref[0, t], ones_rows], axis=0)
        acc_scr[...] = a_scr[...] * acc_scr[...] + _dot(c_aug_t, p_scr[...])

    m_scr[...] = jnp.full(m_scr.shape, MASK_BIAS, F32)
    acc_scr[...] = jnp.zeros(acc_scr.shape, F32)
    pb_scr[...] = jnp.zeros(pb_scr.shape, BF16)
    ab_scr[...] = jnp.ones(ab_scr.shape, F32)
    last_tile = 2 * n_pairs - 1
    scores(0, sa_scr)

    def pair(u, carry):
        t0 = 2 * u
        scores(t0 + 1, sb_scr)
        softmax(t0, sa_scr, pa_scr, aa_scr)
        values(jnp.maximum(t0 - 1, 0), pb_scr, ab_scr)
        scores(jnp.minimum(t0 + 2, last_tile), sa_scr)
        softmax(t0 + 1, sb_scr, pb_scr, ab_scr)
        values(t0, pa_scr, aa_scr)
        return carry

    lax.fori_loop(0, n_pairs, pair, 0)
    values(last_tile, pb_scr, ab_scr)

    inv_l = 1.0 / acc_scr[KV_LATENT:KV_LATENT + 1, :]
    for h in range(ATT_HEADS):
        cols = slice(h * Q_TILE, (h + 1) * Q_TILE)
        o_lat_t = (acc_scr[:KV_LATENT, cols] * inv_l[:, cols]).astype(BF16)
        out_ref[0, :, h * ATT_HEAD_DIM:(h + 1) * ATT_HEAD_DIM] = _dot(wuvt_ref[h], o_lat_t).T


def _sparse_attention(qlt, qi, ki, wit, c, ct, wuvt):
    bsz, seq, _ = c.shape
    assert seq % CNT_TILE == 0 and seq <= MAX_SEQ, seq
    nq = seq // Q_TILE
    n_sel = min(TOPK_MAX, seq // 4)
    cols = ATT_HEADS * Q_TILE
    kernel = functools.partial(_attention_kernel, n_sel=n_sel, idx_bits=max(1, (seq - 1).bit_length()))
    return pl.pallas_call(
        kernel,
        grid=(bsz, nq),
        in_specs=[
            pl.BlockSpec((1, IDX_HEADS, Q_TILE, IDX_DIM), lambda b, j: (b, 0, j, 0)),
            pl.BlockSpec((1, IDX_HEADS, Q_TILE), lambda b, j: (b, 0, j)),
            pl.BlockSpec((1, seq, IDX_DIM), lambda b, j: (b, 0, 0)),
            pl.BlockSpec((1, 1, KV_LATENT, cols), lambda b, j: (b, j, 0, 0)),
            pl.BlockSpec((1, seq, KV_LATENT), lambda b, j: (b, 0, 0)),
            pl.BlockSpec((1, seq // KV_TILE, KV_LATENT, KV_TILE), lambda b, j: (b, 0, 0, 0)),
            pl.BlockSpec(wuvt.shape, lambda b, j: (0, 0, 0)),
        ],
        out_specs=pl.BlockSpec((1, Q_TILE, ATT_WIDTH), lambda b, j: (b, j, 0)),
        out_shape=jax.ShapeDtypeStruct((bsz, seq, ATT_WIDTH), F32),
        scratch_shapes=[
            pltpu.VMEM((seq, Q_TILE), I32),
            pltpu.VMEM((seq, Q_TILE), I16),
            pltpu.VMEM((seq, Q_TILE), I16),
            pltpu.VMEM((seq, Q_TILE), I16),
            pltpu.VMEM((KV_TILE, Q_TILE), F32),
            pltpu.VMEM((KV_TILE, cols), F32),
            pltpu.VMEM((KV_TILE, cols), F32),
            pltpu.VMEM((KV_TILE, cols), BF16),
            pltpu.VMEM((KV_TILE, cols), BF16),
            pltpu.VMEM((1, cols), F32),
            pltpu.VMEM((1, cols), F32),
            pltpu.VMEM((1, cols), F32),
            pltpu.VMEM((KV_LATENT + PACK16, cols), F32),
        ],
        compiler_params=pltpu.CompilerParams(
            dimension_semantics=("arbitrary", "arbitrary"), vmem_limit_bytes=VMEM_LIMIT_BYTES),
        name="sparse_attention",
    )(qi, wit, ki, qlt, c, ct, wuvt)


HALF_STATES = N_STATES // 2
HALF_COLS = 2 * HALF_STATES
SCAN_COLS = 256


def _s5_kernel(u_ref, gs_ref, bmat_ref, a_ref, cmat_ref, dskip_ref, wglu_ref, bglu_ref, out_ref,
               x_scr, state_scr):
    tile, bsz, _ = u_ref.shape
    rows = tile * bsz

    @pl.when(pl.program_id(0) == 0)
    def _():
        state_scr[...] = jnp.zeros(state_scr.shape, F32)

    u = u_ref[...].reshape(rows, SSM_WIDTH)
    ub = u.astype(BF16)
    half_in = SSM_WIDTH // 2
    x_scr[:, :HALF_COLS] = _dot(ub[:, :half_in], bmat_ref[0])
    x_scr[:, HALF_COLS:] = _dot(ub[:, half_in:], bmat_ref[1])

    for half in range(2):
        for off in range(0, HALF_STATES, SCAN_COLS):
            n0 = half * HALF_STATES + off
            cr = half * HALF_COLS + off
            ci = cr + HALF_STATES
            a_re = jnp.broadcast_to(a_ref[0:1, n0:n0 + SCAN_COLS], (bsz, SCAN_COLS))
            a_im = jnp.broadcast_to(a_ref[1:2, n0:n0 + SCAN_COLS], (bsz, SCAN_COLS))

            def step(t, carry, cr=cr, ci=ci, a_re=a_re, a_im=a_im):
                x_re, x_im = carry
                r0 = pl.multiple_of(t * bsz, bsz)
                n_re = a_re * x_re - a_im * x_im + x_scr[pl.ds(r0, bsz), cr:cr + SCAN_COLS]
                n_im = a_re * x_im + a_im * x_re + x_scr[pl.ds(r0, bsz), ci:ci + SCAN_COLS]
                x_scr[pl.ds(r0, bsz), cr:cr + SCAN_COLS] = n_re
                x_scr[pl.ds(r0, bsz), ci:ci + SCAN_COLS] = n_im
                return n_re, n_im

            x_re, x_im = lax.fori_loop(
                0, tile, step,
                (state_scr[:, cr:cr + SCAN_COLS], state_scr[:, ci:ci + SCAN_COLS]), unroll=4)
            state_scr[:, cr:cr + SCAN_COLS] = x_re
            state_scr[:, ci:ci + SCAN_COLS] = x_im

    y0 = _dot(x_scr[:, :HALF_COLS].astype(BF16), cmat_ref[0])
    y1 = _dot(x_scr[:, HALF_COLS:].astype(BF16), cmat_ref[1])
    y = jnp.concatenate([y0, y1], axis=1) + dskip_ref[...] * u
    y = _gelu_tanh(y)
    hcat = _dot(y.astype(BF16), wglu_ref[...]) + bglu_ref[...]
    ssm = hcat[:, :SSM_WIDTH] * _sigmoid(hcat[:, SSM_WIDTH:])
    gs = gs_ref[...].reshape(rows, SSM_WIDTH)
    out_ref[...] = (ssm * _silu(gs)).reshape(tile, bsz, SSM_WIDTH)


def _s5_branch(u_t, gs_t, bmat, a_bar, cmat, dskip, wglu, bglu, tile):
    seq, bsz, _ = u_t.shape
    const2 = lambda i: (0, 0)
    const3 = lambda i: (0, 0, 0)
    return pl.pallas_call(
        _s5_kernel,
        grid=(seq // tile,),
        in_specs=[
            pl.BlockSpec((tile, bsz, SSM_WIDTH), lambda i: (i, 0, 0)),
            pl.BlockSpec((tile, bsz, SSM_WIDTH), lambda i: (i, 0, 0)),
            pl.BlockSpec(bmat.shape, const3),
            pl.BlockSpec(a_bar.shape, const2),
            pl.BlockSpec(cmat.shape, const3),
            pl.BlockSpec(dskip.shape, const2),
            pl.BlockSpec(wglu.shape, const2),
            pl.BlockSpec(bglu.shape, const2),
        ],
        out_specs=pl.BlockSpec((tile, bsz, SSM_WIDTH), lambda i: (i, 0, 0)),
        out_shape=jax.ShapeDtypeStruct((seq, bsz, SSM_WIDTH), F32),
        scratch_shapes=[
            pltpu.VMEM((tile * bsz, 2 * N_STATES), F32),
            pltpu.VMEM((bsz, 2 * N_STATES), F32),
        ],
        compiler_params=pltpu.CompilerParams(
            dimension_semantics=("arbitrary",), vmem_limit_bytes=VMEM_LIMIT_BYTES),
        name="s5_branch",
    )(u_t, gs_t, bmat, a_bar, cmat, dskip, wglu, bglu)


def _s5_matrices(log_dt, a_re, a_im, b_re, b_im, c_re, c_im):
    dt = jnp.exp(log_dt.astype(F32))[:, None]
    lam_re, lam_im = a_re.astype(F32), a_im.astype(F32)
    mag = jnp.exp(lam_re * dt)
    ab_re = mag * jnp.cos(lam_im * dt)
    ab_im = mag * jnp.sin(lam_im * dt)
    den = lam_re * lam_re + lam_im * lam_im
    k_re = ((ab_re - 1.0) * lam_re + ab_im * lam_im) / den
    k_im = (ab_im * lam_re - (ab_re - 1.0) * lam_im) / den
    bb_re = k_re[..., None] * b_re - k_im[..., None] * b_im
    bb_im = k_re[..., None] * b_im + k_im[..., None] * b_re
    gh = N_GROUPS // 2
    eye = jnp.eye(gh, dtype=F32)

    def in_block(bb):
        return jnp.einsum('gpc,gh->gchp', bb, eye).reshape(gh * GROUP_CH, gh * STATE)

    def out_block(cc):
        return jnp.einsum('gcp,gh->gphc', cc, eye).reshape(gh * STATE, gh * GROUP_CH)

    bmat = jnp.stack([
        jnp.concatenate([in_block(bb_re[k * gh:(k + 1) * gh]), in_block(bb_im[k * gh:(k + 1) * gh])], axis=1)
        for k in range(2)])
    cmat = jnp.stack([
        jnp.concatenate([out_block(c_re[k * gh:(k + 1) * gh]), out_block(-c_im[k * gh:(k + 1) * gh])], axis=0)
        for k in range(2)])
    a_bar = jnp.stack([ab_re.reshape(N_STATES), ab_im.reshape(N_STATES)])
    return bmat.astype(BF16), a_bar, cmat.astype(BF16)


def _outproj_kernel(att_ref, ga_ref, ssm_ref, x_ref, wo_ref, lng_ref, lnb_ref, out_ref, *, alpha):
    att = att_ref[0] * _silu(ga_ref[0])
    y = _dot(att.astype(BF16), wo_ref[:ATT_WIDTH, :]) + _dot(ssm_ref[...].astype(BF16), wo_ref[ATT_WIDTH:, :])
    z = alpha * x_ref[0] + y
    mu = jnp.mean(z, axis=-1, keepdims=True)
    zc = z - mu
    var = jnp.mean(zc * zc, axis=-1, keepdims=True)
    out_ref[0] = zc * lax.rsqrt(var + LN_EPS) * lng_ref[...] + lnb_ref[...]


def _out_projection(att, ga, ssm_t, x, wo, lng, lnb, alpha, tile):
    bsz, seq, d = x.shape
    const2 = lambda b, i: (0, 0)
    return pl.pallas_call(
        functools.partial(_outproj_kernel, alpha=alpha),
        grid=(bsz, seq // tile),
        in_specs=[
            pl.BlockSpec((1, tile, ATT_WIDTH), lambda b, i: (b, i, 0)),
            pl.BlockSpec((1, tile, ATT_WIDTH), lambda b, i: (b, i, 0)),
            pl.BlockSpec((tile, SSM_WIDTH), lambda b, i: (i, b)),
            pl.BlockSpec((1, tile, d), lambda b, i: (b, i, 0)),
            pl.BlockSpec(wo.shape, const2),
            pl.BlockSpec(lng.shape, const2),
            pl.BlockSpec(lnb.shape, const2),
        ],
        out_specs=pl.BlockSpec((1, tile, d), lambda b, i: (b, i, 0)),
        out_shape=jax.ShapeDtypeStruct((bsz, seq, d), F32),
        compiler_params=pltpu.CompilerParams(
            dimension_semantics=("arbitrary", "arbitrary"), vmem_limit_bytes=VMEM_LIMIT_BYTES),
        name="out_projection",
    )(att, ga, ssm_t, x, wo, lng, lnb)


def _split_w_in(w_in):
    sizes = (ATT_WIDTH, KV_LATENT, IDX_HEADS * IDX_DIM, IDX_DIM, IDX_HEADS, ATT_WIDTH, SSM_WIDTH, SSM_WIDTH)
    parts, acc = [], 0
    for s in sizes:
        parts.append(w_in[:, acc:acc + s])
        acc += s
    wq, wckv, wqi, wki, wwi, wga, wu, wgs = parts
    pad = jnp.zeros((w_in.shape[0], 2 * LANES - KV_LATENT - IDX_DIM - IDX_HEADS), w_in.dtype)
    wsm = jnp.concatenate([wckv, wki, wwi, pad], axis=1)
    return [w.astype(BF16) for w in (wq, wsm, wqi, wga, wu, wgs)]


def _layer(x, alpha, w_in, kv_g, w_uk, w_uv, log_dt, a_re, a_im, b_re, b_im, c_re, c_im, d_skip,
           w_glu, b_glu, w_out, ln_g, ln_b):
    bsz, seq, d = x.shape
    row_tile = min(512, seq)
    scan_tile = min(32, seq)
    wq, wsm, wqi, wga, wu, wgs = _split_w_in(w_in)
    wukt = jnp.swapaxes(w_uk, 1, 2).astype(BF16)
    qlt, qi, ki, wit, c, ct, ga, u_t, gs_t = _in_projection(
        x, wq, wsm, wqi, wga, wu, wgs, wukt, kv_g.reshape(1, KV_LATENT).astype(F32), row_tile)
    att = _sparse_attention(qlt, qi, ki, wit, c, ct, jnp.swapaxes(w_uv, 1, 2).astype(BF16))
    bmat, a_bar, cmat = _s5_matrices(log_dt, a_re, a_im, b_re, b_im, c_re, c_im)
    ssm_t = _s5_branch(
        u_t.reshape(seq, bsz, SSM_WIDTH), gs_t.reshape(seq, bsz, SSM_WIDTH), bmat, a_bar, cmat,
        d_skip.reshape(1, SSM_WIDTH).astype(F32), w_glu.astype(BF16),
        b_glu.reshape(1, 2 * SSM_WIDTH).astype(F32), scan_tile)
    return _out_projection(
        att, ga, ssm_t.reshape(seq, bsz * SSM_WIDTH), x, w_out.astype(BF16),
        ln_g.reshape(1, d).astype(F32), ln_b.reshape(1, d).astype(F32), alpha, row_tile)


def kernel(x, w_in, kv_norm_g, w_uk, w_uv, log_dt, a_re, a_im, b_re, b_im, c_re, c_im, d_skip, w_glu, b_glu, w_out, ln_g, ln_b):
    depth = w_in.shape[0]
    alpha = (2 * depth) ** 0.25
    h = x
    for l in range(depth):
        h = _layer(h, alpha, w_in[l], kv_norm_g[l], w_uk[l], w_uv[l], log_dt[l], a_re[l], a_im[l],
                   b_re[l], b_im[l], c_re[l], c_im[l], d_skip[l], w_glu[l], b_glu[l],
                   w_out[l], ln_g[l], ln_b[l])
    return h
```

```python
import functools
import math

import jax
import jax.numpy as jnp
from jax import lax
from jax.experimental import pallas as pl
from jax.experimental.pallas import tpu as pltpu

ATT_HEADS = 8
ATT_HEAD_DIM = 64
ATT_WIDTH = ATT_HEADS * ATT_HEAD_DIM
KV_LATENT = 128
IDX_HEADS = 8
IDX_DIM = 32
TOPK_MAX = 256
SSM_WIDTH = 512
GROUP_CH = 16
N_GROUPS = SSM_WIDTH // GROUP_CH
STATE = 64
N_STATES = N_GROUPS * STATE
LN_EPS = 1e-5
RMS_EPS = 1e-6

LANES = 128
SUBLANES = 8
VMEM_LIMIT_BYTES = 56 * 1024 * 1024

Q_TILE = 256
RADIX_BITS = 1
KEY_TILE = 128
KV_TILE = 2 * KEY_TILE
INT_MIN = -(2 ** 31)
MASK_BIAS = -1e30
LOG2E = 1.4426950408889634

F32 = jnp.float32
BF16 = jnp.bfloat16
I32 = jnp.int32

_NT = (((1,), (1,)), ((), ()))


def _dot(a, b):
    return jnp.dot(a, b, preferred_element_type=F32)


def _dot_nt(a, b):
    return lax.dot_general(a, b, _NT, preferred_element_type=F32)


def _sigmoid(x):
    return 1.0 / (1.0 + jnp.exp(-x))


def _silu(x):
    return x * _sigmoid(x)


def _gelu_tanh(x):
    return 0.5 * x * (1.0 + jnp.tanh(math.sqrt(2.0 / math.pi) * (x + 0.044715 * (x * x * x))))


def _inproj_kernel(x_ref, wq_ref, wsm_ref, wqi_ref, wga_ref, wu_ref, wgs_ref, wukt_ref, kvg_ref,
                   qlt_ref, qi_ref, ki_ref, wit_ref, c_ref, ct_ref, ga_ref, u_ref, gs_ref, *, q_tile):
    x = x_ref[0].astype(BF16)
    q = _dot(x, wq_ref[...])
    q_scale = (ATT_HEAD_DIM ** -0.5) * LOG2E
    for h in range(ATT_HEADS):
        qh = q[:, h * ATT_HEAD_DIM:(h + 1) * ATT_HEAD_DIM].astype(BF16)
        ql_t = (_dot(qh, wukt_ref[h]) * q_scale).T.astype(BF16)
        for g in range(x.shape[0] // q_tile):
            qlt_ref[0, g, :, h * q_tile:(h + 1) * q_tile] = ql_t[:, g * q_tile:(g + 1) * q_tile]
    sm = _dot(x, wsm_ref[...])
    ckv = sm[:, :KV_LATENT]
    c = ckv * lax.rsqrt(jnp.mean(ckv * ckv, axis=-1, keepdims=True) + RMS_EPS) * kvg_ref[...]
    c_ref[0] = c.astype(BF16)
    c_t = c.T.astype(BF16)
    for g in range(x.shape[0] // KV_TILE):
        ct_ref[0, g] = c_t[:, g * KV_TILE:(g + 1) * KV_TILE]
    ki_ref[0] = sm[:, KV_LATENT:KV_LATENT + IDX_DIM].astype(BF16)
    sm_t = sm[:, KV_LATENT:].T
    wit_ref[0] = sm_t[IDX_DIM:IDX_DIM + IDX_HEADS, :] * (IDX_HEADS ** -0.5)
    qi = _dot(x, wqi_ref[...]) * (IDX_DIM ** -0.5)
    for h in range(IDX_HEADS):
        qi_ref[0, h] = qi[:, h * IDX_DIM:(h + 1) * IDX_DIM].astype(BF16)
    ga_ref[0] = _dot(x, wga_ref[...])
    u_ref[...] = _dot(x, wu_ref[...])
    gs_ref[...] = _dot(x, wgs_ref[...])


def _in_projection(x, wq, wsm, wqi, wga, wu, wgs, wukt, kvg, tile, q_tile):
    bsz, seq, d = x.shape
    nt = seq // tile
    const2 = lambda b, i: (0, 0)
    const3 = lambda b, i: (0, 0, 0)
    out_shape = (
        jax.ShapeDtypeStruct((bsz, seq // q_tile, KV_LATENT, ATT_HEADS * q_tile), BF16),
        jax.ShapeDtypeStruct((bsz, IDX_HEADS, seq, IDX_DIM), BF16),
        jax.ShapeDtypeStruct((bsz, seq, IDX_DIM), BF16),
        jax.ShapeDtypeStruct((bsz, IDX_HEADS, seq), F32),
        jax.ShapeDtypeStruct((bsz, seq, KV_LATENT), BF16),
        jax.ShapeDtypeStruct((bsz, seq // KV_TILE, KV_LATENT, KV_TILE), BF16),
        jax.ShapeDtypeStruct((bsz, seq, ATT_WIDTH), F32),
        jax.ShapeDtypeStruct((seq, bsz * SSM_WIDTH), F32),
        jax.ShapeDtypeStruct((seq, bsz * SSM_WIDTH), F32),
    )
    out_specs = (
        pl.BlockSpec((1, tile // q_tile, KV_LATENT, ATT_HEADS * q_tile), lambda b, i: (b, i, 0, 0)),
        pl.BlockSpec((1, IDX_HEADS, tile, IDX_DIM), lambda b, i: (b, 0, i, 0)),
        pl.BlockSpec((1, tile, IDX_DIM), lambda b, i: (b, i, 0)),
        pl.BlockSpec((1, IDX_HEADS, tile), lambda b, i: (b, 0, i)),
        pl.BlockSpec((1, tile, KV_LATENT), lambda b, i: (b, i, 0)),
        pl.BlockSpec((1, tile // KV_TILE, KV_LATENT, KV_TILE), lambda b, i: (b, i, 0, 0)),
        pl.BlockSpec((1, tile, ATT_WIDTH), lambda b, i: (b, i, 0)),
        pl.BlockSpec((tile, SSM_WIDTH), lambda b, i: (i, b)),
        pl.BlockSpec((tile, SSM_WIDTH), lambda b, i: (i, b)),
    )
    in_specs = [
        pl.BlockSpec((1, tile, d), lambda b, i: (b, i, 0)),
        pl.BlockSpec(wq.shape, const2),
        pl.BlockSpec(wsm.shape, const2),
        pl.BlockSpec(wqi.shape, const2),
        pl.BlockSpec(wga.shape, const2),
        pl.BlockSpec(wu.shape, const2),
        pl.BlockSpec(wgs.shape, const2),
        pl.BlockSpec(wukt.shape, const3),
        pl.BlockSpec(kvg.shape, const2),
    ]
    return pl.pallas_call(
        functools.partial(_inproj_kernel, q_tile=q_tile),
        grid=(bsz, nt),
        in_specs=in_specs,
        out_specs=out_specs,
        out_shape=out_shape,
        compiler_params=pltpu.CompilerParams(
            dimension_semantics=("arbitrary", "arbitrary"), vmem_limit_bytes=VMEM_LIMIT_BYTES),
        name="in_projection",
    )(x, wq, wsm, wqi, wga, wu, wgs, wukt, kvg)


CNT_TILE = 512
PACK16 = 16
I16 = jnp.int16
I16_MIN = -(2 ** 15)
MAX_SEQ = 256 * PACK16


def _tree_sum(xs):
    xs = list(xs)
    while len(xs) > 1:
        nxt = [xs[i] + xs[i + 1] for i in range(0, len(xs) - 1, 2)]
        if len(xs) % 2:
            nxt.append(xs[-1])
        xs = nxt
    return xs[0]


def _radix_select16(src_scr, n_cnt_tiles, k_target, ge_all, q_tile, bits):
    groups = CNT_TILE // PACK16
    n_cand = 2 ** bits - 1
    one, zero = jnp.ones((), BF16), jnp.zeros((), BF16)

    def bit_pass(i, state):
        tau, c_ge, c_gt = state
        step = jnp.left_shift(jnp.int32(1), 16 - bits * (i + 1))
        cands = [tau + k * step for k in range(1, n_cand + 1)]
        cands16 = [jnp.broadcast_to(c, (PACK16, q_tile)).astype(I16)[None] for c in cands]

        def count_tile(t, accs):
            r0 = pl.multiple_of(t * CNT_TILE, CNT_TILE)
            x = src_scr[pl.ds(r0, CNT_TILE), :].reshape(groups, PACK16, q_tile)
            out = []
            for acc, c16 in zip(accs, cands16):
                ge = jnp.where(x >= c16, one, zero)
                out.append(acc + _tree_sum([ge[r] for r in range(groups)]))
            return tuple(out)

        accs = lax.fori_loop(0, n_cnt_tiles, count_tile,
                             tuple(jnp.zeros((PACK16, q_tile), BF16) for _ in range(n_cand)))
        counts = [jnp.sum(a.astype(F32), axis=0, keepdims=True).astype(I32) for a in accs]
        new_tau, new_ge, new_gt = tau, c_ge, counts[0]
        for k in range(n_cand):
            ok = counts[k] >= k_target
            above = counts[k + 1] if k + 1 < n_cand else c_gt
            new_tau = jnp.where(ok, cands[k], new_tau)
            new_ge = jnp.where(ok, counts[k], new_ge)
            new_gt = jnp.where(ok, above, new_gt)
        return new_tau, new_ge, new_gt

    init = (jnp.full((1, q_tile), I16_MIN, I32), ge_all, jnp.zeros((1, q_tile), I32))
    return lax.fori_loop(0, 16 // bits, bit_pass, init)


def _attention_kernel(qi_ref, wit_ref, ki_ref, qlt_ref, c_ref, ct_ref, wuvt_ref, out_ref,
                      key_scr, khi_scr, klo_scr, ksel_scr, bias_scr,
                      sa_scr, sb_scr, pa_scr, pb_scr, aa_scr, ab_scr, m_scr, acc_scr, *, n_sel, idx_bits, q_tile, bits):
    j = pl.program_id(1)
    n_key_tiles = (j + 1) * (q_tile // KEY_TILE)
    n_cnt_tiles = (n_key_tiles + 3) // 4
    n_pairs = n_cnt_tiles

    q_pos = j * q_tile + lax.broadcasted_iota(I32, (KEY_TILE, q_tile), 1)
    k_off = lax.broadcasted_iota(I32, (KEY_TILE, q_tile), 0)

    def store_keys(r0, key):
        key_scr[pl.ds(r0, KEY_TILE), :] = key
        khi_scr[pl.ds(r0, KEY_TILE), :] = jnp.right_shift(key, 16).astype(I16)
        klo_scr[pl.ds(r0, KEY_TILE), :] = ((key & 0xFFFF) + I16_MIN).astype(I16)

    def score_tiles(i, carry):
        for sub in range(CNT_TILE // KEY_TILE):
            t = i * (CNT_TILE // KEY_TILE) + sub
            r0 = pl.multiple_of(t * KEY_TILE, KEY_TILE)
            ki_t = ki_ref[0, pl.ds(r0, KEY_TILE), :]
            score = jnp.zeros((KEY_TILE, q_tile), F32)
            for h in range(IDX_HEADS):
                logits = _dot_nt(ki_t, qi_ref[0, h])
                score = score + wit_ref[0, h:h + 1, :] * jnp.maximum(logits, 0.0)
            bits32 = pltpu.bitcast(score, I32)
            key = jnp.where(bits32 < 0, bits32 ^ jnp.int32(0x7FFFFFFF), bits32)
            key = jnp.where(score == 0.0, 0, key)
            store_keys(r0, jnp.where(t * KEY_TILE + k_off <= q_pos, key, INT_MIN))
        return carry

    lax.fori_loop(0, n_cnt_tiles, score_tiles, 0)

    k_sel = jnp.full((1, q_tile), n_sel, I32)
    n_rows = jnp.full((1, q_tile), n_cnt_tiles * CNT_TILE, I32)
    tau_hi, ge_hi, gt_hi = _radix_select16(khi_scr, n_cnt_tiles, k_sel, n_rows, q_tile, bits)
    tau_hi16 = jnp.broadcast_to(tau_hi, (PACK16, q_tile)).astype(I16)

    def sel_tile(t, carry):
        r0 = pl.multiple_of(t * CNT_TILE, CNT_TILE)
        hi = khi_scr[pl.ds(r0, CNT_TILE), :].reshape(CNT_TILE // PACK16, PACK16, q_tile)
        lo = klo_scr[pl.ds(r0, CNT_TILE), :].reshape(CNT_TILE // PACK16, PACK16, q_tile)
        sel = jnp.where(hi == tau_hi16[None], lo, jnp.full((), I16_MIN, I16))
        ksel_scr[pl.ds(r0, CNT_TILE), :] = sel.reshape(CNT_TILE, q_tile)
        return carry

    lax.fori_loop(0, n_cnt_tiles, sel_tile, 0)
    tau_lo, ge_lo, gt_lo = _radix_select16(ksel_scr, n_cnt_tiles, k_sel - gt_hi, ge_hi - gt_hi, q_tile, bits)
    tau = jnp.left_shift(tau_hi, 16) | (tau_lo - I16_MIN)
    need = n_sel - (gt_hi + gt_lo)
    excess = ((ge_lo - gt_lo) > need) & (tau != INT_MIN)

    @pl.when(jnp.max(jnp.where(excess, 1, 0)) > 0)
    def _():
        def idx_pass(i, lo_idx):
            cand = lo_idx + jnp.left_shift(jnp.int32(1), idx_bits - 1 - i)

            def count_tile(t, acc):
                r0 = pl.multiple_of(t * KEY_TILE, KEY_TILE)
                eq = jnp.where(key_scr[pl.ds(r0, KEY_TILE), :] == tau,
                               jnp.where(t * KEY_TILE + k_off < cand, 1, 0), 0)
                return acc + jnp.sum(eq.reshape(KEY_TILE // SUBLANES, SUBLANES, q_tile), axis=0)

            acc = lax.fori_loop(0, n_key_tiles, count_tile, jnp.zeros((SUBLANES, q_tile), I32))
            return jnp.where(jnp.sum(acc, axis=0, keepdims=True) < need, cand, lo_idx)

        cut = lax.fori_loop(0, idx_bits, idx_pass, jnp.zeros((1, q_tile), I32))

        def drop_tile(t, carry):
            r0 = pl.multiple_of(t * KEY_TILE, KEY_TILE)
            key = key_scr[pl.ds(r0, KEY_TILE), :]
            late = jnp.where(key == tau, jnp.where(t * KEY_TILE + k_off > cut, 1, 0), 0)
            key_scr[pl.ds(r0, KEY_TILE), :] = jnp.where(late == 1, INT_MIN, key)
            return carry

        lax.fori_loop(0, n_key_tiles, drop_tile, 0)

    ones_rows = jnp.ones((PACK16, KV_TILE), BF16)

    def scores(t, s_scr):
        r0 = pl.multiple_of(t * KV_TILE, KV_TILE)
        s_scr[...] = _dot(c_ref[0, pl.ds(r0, KV_TILE), :], qlt_ref[0, 0])

    def softmax(t, s_scr, p_scr, a_scr):
        for half in range(KV_TILE // KEY_TILE):
            kt = 2 * t + half
            r0 = pl.multiple_of(kt * KEY_TILE, KEY_TILE)
            bias_scr[half * KEY_TILE:(half + 1) * KEY_TILE, :] = jnp.where(
                key_scr[pl.ds(r0, KEY_TILE), :] >= tau,
                jnp.where(kt * KEY_TILE + k_off <= q_pos, 0.0, MASK_BIAS), MASK_BIAS).astype(F32)
        for h in range(ATT_HEADS):
            cols = slice(h * q_tile, (h + 1) * q_tile)
            s = s_scr[:, cols] + bias_scr[...]
            m_prev = m_scr[:, cols]
            m_new = jnp.maximum(m_prev, jnp.max(s, axis=0, keepdims=True))
            a_scr[:, cols] = jnp.exp2(m_prev - m_new)
            p_scr[:, cols] = jnp.exp2(s - m_new).astype(BF16)
            m_scr[:, cols] = m_new

    def values(t, p_scr, a_scr):
        c_aug_t = jnp.concatenate([ct_ref[0, t], ones_rows], axis=0)
        acc_scr[...] = a_scr[...] * acc_scr[...] + _dot(c_aug_t, p_scr[...])

    m_scr[...] = jnp.full(m_scr.shape, MASK_BIAS, F32)
    acc_scr[...] = jnp.zeros(acc_scr.shape, F32)
    pb_scr[...] = jnp.zeros(pb_scr.shape, BF16)
    ab_scr[...] = jnp.ones(ab_scr.shape, F32)
    last_tile = 2 * n_pairs - 1
    scores(0, sa_scr)

    def pair(u, carry):
        t0 = 2 * u
        scores(t0 + 1, sb_scr)
        softmax(t0, sa_scr, pa_scr, aa_scr)
        values(jnp.maximum(t0 - 1, 0), pb_scr, ab_scr)
        scores(jnp.minimum(t0 + 2, last_tile), sa_scr)
        softmax(t0 + 1, sb_scr, pb_scr, ab_scr)
        values(t0, pa_scr, aa_scr)
        return carry

    lax.fori_loop(0, n_pairs, pair, 0)
    values(last_tile, pb_scr, ab_scr)

    inv_l = 1.0 / acc_scr[KV_LATENT:KV_LATENT + 1, :]
    for h in range(ATT_HEADS):
        cols = slice(h * q_tile, (h + 1) * q_tile)
        o_lat_t = (acc_scr[:KV_LATENT, cols] * inv_l[:, cols]).astype(BF16)
        out_ref[0, :, h * ATT_HEAD_DIM:(h + 1) * ATT_HEAD_DIM] = _dot(wuvt_ref[h], o_lat_t).T


def _sparse_attention(qlt, qi, ki, wit, c, ct, wuvt, q_tile=Q_TILE, bits=1):
    bsz, seq, _ = c.shape
    assert seq % CNT_TILE == 0 and seq <= MAX_SEQ, seq
    nq = seq // q_tile
    n_sel = min(TOPK_MAX, seq // 4)
    cols = ATT_HEADS * q_tile
    kernel = functools.partial(_attention_kernel, n_sel=n_sel, idx_bits=max(1, (seq - 1).bit_length()),
                               q_tile=q_tile, bits=bits)
    return pl.pallas_call(
        kernel,
        grid=(bsz, nq),
        in_specs=[
            pl.BlockSpec((1, IDX_HEADS, q_tile, IDX_DIM), lambda b, j: (b, 0, j, 0)),
            pl.BlockSpec((1, IDX_HEADS, q_tile), lambda b, j: (b, 0, j)),
            pl.BlockSpec((1, seq, IDX_DIM), lambda b, j: (b, 0, 0)),
            pl.BlockSpec((1, 1, KV_LATENT, cols), lambda b, j: (b, j, 0, 0)),
            pl.BlockSpec((1, seq, KV_LATENT), lambda b, j: (b, 0, 0)),
            pl.BlockSpec((1, seq // KV_TILE, KV_LATENT, KV_TILE), lambda b, j: (b, 0, 0, 0)),
            pl.BlockSpec(wuvt.shape, lambda b, j: (0, 0, 0)),
        ],
        out_specs=pl.BlockSpec((1, q_tile, ATT_WIDTH), lambda b, j: (b, j, 0)),
        out_shape=jax.ShapeDtypeStruct((bsz, seq, ATT_WIDTH), F32),
        scratch_shapes=[
            pltpu.VMEM((seq, q_tile), I32),
            pltpu.VMEM((seq, q_tile), I16),
            pltpu.VMEM((seq, q_tile), I16),
            pltpu.VMEM((seq, q_tile), I16),
            pltpu.VMEM((KV_TILE, q_tile), F32),
            pltpu.VMEM((KV_TILE, cols), F32),
            pltpu.VMEM((KV_TILE, cols), F32),
            pltpu.VMEM((KV_TILE, cols), BF16),
            pltpu.VMEM((KV_TILE, cols), BF16),
            pltpu.VMEM((1, cols), F32),
            pltpu.VMEM((1, cols), F32),
            pltpu.VMEM((1, cols), F32),
            pltpu.VMEM((KV_LATENT + PACK16, cols), F32),
        ],
        compiler_params=pltpu.CompilerParams(
            dimension_semantics=("arbitrary", "arbitrary"), vmem_limit_bytes=VMEM_LIMIT_BYTES),
        name="sparse_attention",
    )(qi, wit, ki, qlt, c, ct, wuvt)


HALF_STATES = N_STATES // 2
HALF_COLS = 2 * HALF_STATES
SCAN_COLS = 256


def _s5_kernel(u_ref, gs_ref, bmat_ref, a_ref, cmat_ref, dskip_ref, wglu_ref, bglu_ref, out_ref,
               x_scr, state_scr):
    tile, bsz, _ = u_ref.shape
    rows = tile * bsz

    @pl.when(pl.program_id(0) == 0)
    def _():
        state_scr[...] = jnp.zeros(state_scr.shape, F32)

    u = u_ref[...].reshape(rows, SSM_WIDTH)
    ub = u.astype(BF16)
    half_in = SSM_WIDTH // 2
    x_scr[:, :HALF_COLS] = _dot(ub[:, :half_in], bmat_ref[0])
    x_scr[:, HALF_COLS:] = _dot(ub[:, half_in:], bmat_ref[1])

    for half in range(2):
        for off in range(0, HALF_STATES, SCAN_COLS):
            n0 = half * HALF_STATES + off
            cr = half * HALF_COLS + off
            ci = cr + HALF_STATES
            a_re = jnp.broadcast_to(a_ref[0:1, n0:n0 + SCAN_COLS], (bsz, SCAN_COLS))
            a_im = jnp.broadcast_to(a_ref[1:2, n0:n0 + SCAN_COLS], (bsz, SCAN_COLS))

            def step(t, carry, cr=cr, ci=ci, a_re=a_re, a_im=a_im):
                x_re, x_im = carry
                r0 = pl.multiple_of(t * bsz, bsz)
                n_re = a_re * x_re - a_im * x_im + x_scr[pl.ds(r0, bsz), cr:cr + SCAN_COLS]
                n_im = a_re * x_im + a_im * x_re + x_scr[pl.ds(r0, bsz), ci:ci + SCAN_COLS]
                x_scr[pl.ds(r0, bsz), cr:cr + SCAN_COLS] = n_re
                x_scr[pl.ds(r0, bsz), ci:ci + SCAN_COLS] = n_im
                return n_re, n_im

            x_re, x_im = lax.fori_loop(
                0, tile, step,
                (state_scr[:, cr:cr + SCAN_COLS], state_scr[:, ci:ci + SCAN_COLS]), unroll=4)
            state_scr[:, cr:cr + SCAN_COLS] = x_re
            state_scr[:, ci:ci + SCAN_COLS] = x_im

    y0 = _dot(x_scr[:, :HALF_COLS].astype(BF16), cmat_ref[0])
    y1 = _dot(x_scr[:, HALF_COLS:].astype(BF16), cmat_ref[1])
    y = jnp.concatenate([y0, y1], axis=1) + dskip_ref[...] * u
    y = _gelu_tanh(y)
    hcat = _dot(y.astype(BF16), wglu_ref[...]) + bglu_ref[...]
    ssm = hcat[:, :SSM_WIDTH] * _sigmoid(hcat[:, SSM_WIDTH:])
    gs = gs_ref[...].reshape(rows, SSM_WIDTH)
    out_ref[...] = (ssm * _silu(gs)).reshape(tile, bsz, SSM_WIDTH)


def _s5_branch(u_t, gs_t, bmat, a_bar, cmat, dskip, wglu, bglu, tile):
    seq, bsz, _ = u_t.shape
    const2 = lambda i: (0, 0)
    const3 = lambda i: (0, 0, 0)
    return pl.pallas_call(
        _s5_kernel,
        grid=(seq // tile,),
        in_specs=[
            pl.BlockSpec((tile, bsz, SSM_WIDTH), lambda i: (i, 0, 0)),
            pl.BlockSpec((tile, bsz, SSM_WIDTH), lambda i: (i, 0, 0)),
            pl.BlockSpec(bmat.shape, const3),
            pl.BlockSpec(a_bar.shape, const2),
            pl.BlockSpec(cmat.shape, const3),
            pl.BlockSpec(dskip.shape, const2),
            pl.BlockSpec(wglu.shape, const2),
            pl.BlockSpec(bglu.shape, const2),
        ],
        out_specs=pl.BlockSpec((tile, bsz, SSM_WIDTH), lambda i: (i, 0, 0)),
        out_shape=jax.ShapeDtypeStruct((seq, bsz, SSM_WIDTH), F32),
        scratch_shapes=[
            pltpu.VMEM((tile * bsz, 2 * N_STATES), F32),
            pltpu.VMEM((bsz, 2 * N_STATES), F32),
        ],
        compiler_params=pltpu.CompilerParams(
            dimension_semantics=("arbitrary",), vmem_limit_bytes=VMEM_LIMIT_BYTES),
        name="s5_branch",
    )(u_t, gs_t, bmat, a_bar, cmat, dskip, wglu, bglu)


def _s5_matrices(log_dt, a_re, a_im, b_re, b_im, c_re, c_im):
    dt = jnp.exp(log_dt.astype(F32))[:, None]
    lam_re, lam_im = a_re.astype(F32), a_im.astype(F32)
    mag = jnp.exp(lam_re * dt)
    ab_re = mag * jnp.cos(lam_im * dt)
    ab_im = mag * jnp.sin(lam_im * dt)
    den = lam_re * lam_re + lam_im * lam_im
    k_re = ((ab_re - 1.0) * lam_re + ab_im * lam_im) / den
    k_im = (ab_im * lam_re - (ab_re - 1.0) * lam_im) / den
    bb_re = k_re[..., None] * b_re - k_im[..., None] * b_im
    bb_im = k_re[..., None] * b_im + k_im[..., None] * b_re
    gh = N_GROUPS // 2
    eye = jnp.eye(gh, dtype=F32)

    def in_block(bb):
        return jnp.einsum('gpc,gh->gchp', bb, eye).reshape(gh * GROUP_CH, gh * STATE)

    def out_block(cc):
        return jnp.einsum('gcp,gh->gphc', cc, eye).reshape(gh * STATE, gh * GROUP_CH)

    bmat = jnp.stack([
        jnp.concatenate([in_block(bb_re[k * gh:(k + 1) * gh]), in_block(bb_im[k * gh:(k + 1) * gh])], axis=1)
        for k in range(2)])
    cmat = jnp.stack([
        jnp.concatenate([out_block(c_re[k * gh:(k + 1) * gh]), out_block(-c_im[k * gh:(k + 1) * gh])], axis=0)
        for k in range(2)])
    a_bar = jnp.stack([ab_re.reshape(N_STATES), ab_im.reshape(N_STATES)])
    return bmat.astype(BF16), a_bar, cmat.astype(BF16)


def _outproj_kernel(att_ref, ga_ref, ssm_ref, x_ref, wo_ref, lng_ref, lnb_ref, out_ref, *, alpha):
    att = att_ref[0] * _silu(ga_ref[0])
    y = _dot(att.astype(BF16), wo_ref[:ATT_WIDTH, :]) + _dot(ssm_ref[...].astype(BF16), wo_ref[ATT_WIDTH:, :])
    z = alpha * x_ref[0] + y
    mu = jnp.mean(z, axis=-1, keepdims=True)
    zc = z - mu
    var = jnp.mean(zc * zc, axis=-1, keepdims=True)
    out_ref[0] = zc * lax.rsqrt(var + LN_EPS) * lng_ref[...] + lnb_ref[...]


def _out_projection(att, ga, ssm_t, x, wo, lng, lnb, alpha, tile):
    bsz, seq, d = x.shape
    const2 = lambda b, i: (0, 0)
    return pl.pallas_call(
        functools.partial(_outproj_kernel, alpha=alpha),
        grid=(bsz, seq // tile),
        in_specs=[
            pl.BlockSpec((1, tile, ATT_WIDTH), lambda b, i: (b, i, 0)),
            pl.BlockSpec((1, tile, ATT_WIDTH), lambda b, i: (b, i, 0)),
            pl.BlockSpec((tile, SSM_WIDTH), lambda b, i: (i, b)),
            pl.BlockSpec((1, tile, d), lambda b, i: (b, i, 0)),
            pl.BlockSpec(wo.shape, const2),
            pl.BlockSpec(lng.shape, const2),
            pl.BlockSpec(lnb.shape, const2),
        ],
        out_specs=pl.BlockSpec((1, tile, d), lambda b, i: (b, i, 0)),
        out_shape=jax.ShapeDtypeStruct((bsz, seq, d), F32),
        compiler_params=pltpu.CompilerParams(
            dimension_semantics=("arbitrary", "arbitrary"), vmem_limit_bytes=VMEM_LIMIT_BYTES),
        name="out_projection",
    )(att, ga, ssm_t, x, wo, lng, lnb)


def _split_w_in(w_in):
    sizes = (ATT_WIDTH, KV_LATENT, IDX_HEADS * IDX_DIM, IDX_DIM, IDX_HEADS, ATT_WIDTH, SSM_WIDTH, SSM_WIDTH)
    parts, acc = [], 0
    for s in sizes:
        parts.append(w_in[:, acc:acc + s])
        acc += s
    wq, wckv, wqi, wki, wwi, wga, wu, wgs = parts
    pad = jnp.zeros((w_in.shape[0], 2 * LANES - KV_LATENT - IDX_DIM - IDX_HEADS), w_in.dtype)
    wsm = jnp.concatenate([wckv, wki, wwi, pad], axis=1)
    return [w.astype(BF16) for w in (wq, wsm, wqi, wga, wu, wgs)]


def _layer(x, alpha, cfg, w_in, kv_g, w_uk, w_uv, log_dt, a_re, a_im, b_re, b_im, c_re, c_im, d_skip,
           w_glu, b_glu, w_out, ln_g, ln_b):
    bsz, seq, d = x.shape
    row_tile = min(512, seq)
    scan_tile = min(32, seq)
    wq, wsm, wqi, wga, wu, wgs = _split_w_in(w_in)
    wukt = jnp.swapaxes(w_uk, 1, 2).astype(BF16)
    qlt, qi, ki, wit, c, ct, ga, u_t, gs_t = _in_projection(
        x, wq, wsm, wqi, wga, wu, wgs, wukt, kv_g.reshape(1, KV_LATENT).astype(F32), row_tile, cfg[0])
    att = _sparse_attention(qlt, qi, ki, wit, c, ct, jnp.swapaxes(w_uv, 1, 2).astype(BF16), cfg[0], cfg[1])
    bmat, a_bar, cmat = _s5_matrices(log_dt, a_re, a_im, b_re, b_im, c_re, c_im)
    ssm_t = _s5_branch(
        u_t.reshape(seq, bsz, SSM_WIDTH), gs_t.reshape(seq, bsz, SSM_WIDTH), bmat, a_bar, cmat,
        d_skip.reshape(1, SSM_WIDTH).astype(F32), w_glu.astype(BF16),
        b_glu.reshape(1, 2 * SSM_WIDTH).astype(F32), scan_tile)
    return _out_projection(
        att, ga, ssm_t.reshape(seq, bsz * SSM_WIDTH), x, w_out.astype(BF16),
        ln_g.reshape(1, d).astype(F32), ln_b.reshape(1, d).astype(F32), alpha, row_tile)


def kernel(x, w_in, kv_norm_g, w_uk, w_uv, log_dt, a_re, a_im, b_re, b_im, c_re, c_im, d_skip, w_glu, b_glu, w_out, ln_g, ln_b):
    depth = w_in.shape[0]
    alpha = (2 * depth) ** 0.25
    h = x
    for l in range(depth):
        h = _layer(h, alpha, (Q_TILE, RADIX_BITS), w_in[l], kv_norm_g[l], w_uk[l], w_uv[l], log_dt[l], a_re[l], a_im[l],
                   b_re[l], b_im[l], c_re[l], c_im[l], d_skip[l], w_glu[l], b_glu[l],
                   w_out[l], ln_g[l], ln_b[l])
    return h
```

```python
import functools
import math

import jax
import jax.numpy as jnp
from jax import lax
from jax.experimental import pallas as pl
from jax.experimental.pallas import tpu as pltpu

ATT_HEADS = 8
ATT_HEAD_DIM = 64
ATT_WIDTH = ATT_HEADS * ATT_HEAD_DIM
KV_LATENT = 128
IDX_HEADS = 8
IDX_DIM = 32
TOPK_MAX = 256
SSM_WIDTH = 512
GROUP_CH = 16
N_GROUPS = SSM_WIDTH // GROUP_CH
STATE = 64
N_STATES = N_GROUPS * STATE
LN_EPS = 1e-5
RMS_EPS = 1e-6

LANES = 128
SUBLANES = 8
VMEM_LIMIT_BYTES = 56 * 1024 * 1024

Q_TILE = 256
RADIX_BITS = 1
ROW_TILE = 1024
SCAN_TILE = 64
KEY_TILE = 128
KV_TILE = 2 * KEY_TILE
INT_MIN = -(2 ** 31)
MASK_BIAS = -1e30
LOG2E = 1.4426950408889634

F32 = jnp.float32
BF16 = jnp.bfloat16
I32 = jnp.int32

_NT = (((1,), (1,)), ((), ()))


def _dot(a, b):
    return jnp.dot(a, b, preferred_element_type=F32)


def _dot_nt(a, b):
    return lax.dot_general(a, b, _NT, preferred_element_type=F32)


def _sigmoid(x):
    return 1.0 / (1.0 + jnp.exp(-x))


def _silu(x):
    return x * _sigmoid(x)


def _gelu_tanh(x):
    return 0.5 * x * (1.0 + jnp.tanh(math.sqrt(2.0 / math.pi) * (x + 0.044715 * (x * x * x))))


def _inproj_kernel(x_ref, wq_ref, wsm_ref, wqi_ref, wga_ref, wu_ref, wgs_ref, wukt_ref, kvg_ref,
                   qlt_ref, qi_ref, ki_ref, wit_ref, c_ref, ct_ref, ga_ref, u_ref, gs_ref, *, q_tile):
    x = x_ref[0].astype(BF16)
    q = _dot(x, wq_ref[...])
    q_scale = (ATT_HEAD_DIM ** -0.5) * LOG2E
    for h in range(ATT_HEADS):
        qh = q[:, h * ATT_HEAD_DIM:(h + 1) * ATT_HEAD_DIM].astype(BF16)
        ql_t = (_dot(qh, wukt_ref[h]) * q_scale).T.astype(BF16)
        for g in range(x.shape[0] // q_tile):
            qlt_ref[0, g, :, h * q_tile:(h + 1) * q_tile] = ql_t[:, g * q_tile:(g + 1) * q_tile]
    sm = _dot(x, wsm_ref[...])
    ckv = sm[:, :KV_LATENT]
    c = ckv * lax.rsqrt(jnp.mean(ckv * ckv, axis=-1, keepdims=True) + RMS_EPS) * kvg_ref[...]
    c_ref[0] = c.astype(BF16)
    c_t = c.T.astype(BF16)
    for g in range(x.shape[0] // KV_TILE):
        ct_ref[0, g] = c_t[:, g * KV_TILE:(g + 1) * KV_TILE]
    ki_ref[0] = sm[:, KV_LATENT:KV_LATENT + IDX_DIM].astype(BF16)
    sm_t = sm[:, KV_LATENT:].T
    wit_ref[0] = sm_t[IDX_DIM:IDX_DIM + IDX_HEADS, :] * (IDX_HEADS ** -0.5)
    qi = _dot(x, wqi_ref[...]) * (IDX_DIM ** -0.5)
    for h in range(IDX_HEADS):
        qi_ref[0, h] = qi[:, h * IDX_DIM:(h + 1) * IDX_DIM].astype(BF16)
    ga_ref[0] = _dot(x, wga_ref[...])
    u_ref[...] = _dot(x, wu_ref[...])
    gs_ref[...] = _dot(x, wgs_ref[...])


def _in_projection(x, wq, wsm, wqi, wga, wu, wgs, wukt, kvg, tile, q_tile):
    bsz, seq, d = x.shape
    nt = seq // tile
    const2 = lambda b, i: (0, 0)
    const3 = lambda b, i: (0, 0, 0)
    out_shape = (
        jax.ShapeDtypeStruct((bsz, seq // q_tile, KV_LATENT, ATT_HEADS * q_tile), BF16),
        jax.ShapeDtypeStruct((bsz, IDX_HEADS, seq, IDX_DIM), BF16),
        jax.ShapeDtypeStruct((bsz, seq, IDX_DIM), BF16),
        jax.ShapeDtypeStruct((bsz, IDX_HEADS, seq), F32),
        jax.ShapeDtypeStruct((bsz, seq, KV_LATENT), BF16),
        jax.ShapeDtypeStruct((bsz, seq // KV_TILE, KV_LATENT, KV_TILE), BF16),
        jax.ShapeDtypeStruct((bsz, seq, ATT_WIDTH), F32),
        jax.ShapeDtypeStruct((seq, bsz * SSM_WIDTH), F32),
        jax.ShapeDtypeStruct((seq, bsz * SSM_WIDTH), F32),
    )
    out_specs = (
        pl.BlockSpec((1, tile // q_tile, KV_LATENT, ATT_HEADS * q_tile), lambda b, i: (b, i, 0, 0)),
        pl.BlockSpec((1, IDX_HEADS, tile, IDX_DIM), lambda b, i: (b, 0, i, 0)),
        pl.BlockSpec((1, tile, IDX_DIM), lambda b, i: (b, i, 0)),
        pl.BlockSpec((1, IDX_HEADS, tile), lambda b, i: (b, 0, i)),
        pl.BlockSpec((1, tile, KV_LATENT), lambda b, i: (b, i, 0)),
        pl.BlockSpec((1, tile // KV_TILE, KV_LATENT, KV_TILE), lambda b, i: (b, i, 0, 0)),
        pl.BlockSpec((1, tile, ATT_WIDTH), lambda b, i: (b, i, 0)),
        pl.BlockSpec((tile, SSM_WIDTH), lambda b, i: (i, b)),
        pl.BlockSpec((tile, SSM_WIDTH), lambda b, i: (i, b)),
    )
    in_specs = [
        pl.BlockSpec((1, tile, d), lambda b, i: (b, i, 0)),
        pl.BlockSpec(wq.shape, const2),
        pl.BlockSpec(wsm.shape, const2),
        pl.BlockSpec(wqi.shape, const2),
        pl.BlockSpec(wga.shape, const2),
        pl.BlockSpec(wu.shape, const2),
        pl.BlockSpec(wgs.shape, const2),
        pl.BlockSpec(wukt.shape, const3),
        pl.BlockSpec(kvg.shape, const2),
    ]
    return pl.pallas_call(
        functools.partial(_inproj_kernel, q_tile=q_tile),
        grid=(bsz, nt),
        in_specs=in_specs,
        out_specs=out_specs,
        out_shape=out_shape,
        compiler_params=pltpu.CompilerParams(
            dimension_semantics=("arbitrary", "arbitrary"), vmem_limit_bytes=VMEM_LIMIT_BYTES),
        name="in_projection",
    )(x, wq, wsm, wqi, wga, wu, wgs, wukt, kvg)


CNT_TILE = 512
PACK16 = 16
I16 = jnp.int16
I16_MIN = -(2 ** 15)
MAX_SEQ = 256 * PACK16


def _tree_sum(xs):
    xs = list(xs)
    while len(xs) > 1:
        nxt = [xs[i] + xs[i + 1] for i in range(0, len(xs) - 1, 2)]
        if len(xs) % 2:
            nxt.append(xs[-1])
        xs = nxt
    return xs[0]


def _radix_select16(src_scr, n_cnt_tiles, k_target, ge_all, q_tile, bits):
    groups = CNT_TILE // PACK16
    n_cand = 2 ** bits - 1
    one, zero = jnp.ones((), BF16), jnp.zeros((), BF16)

    def bit_pass(i, state):
        tau, c_ge, c_gt = state
        step = jnp.left_shift(jnp.int32(1), 16 - bits * (i + 1))
        cands = [tau + k * step for k in range(1, n_cand + 1)]
        cands16 = [jnp.broadcast_to(c, (PACK16, q_tile)).astype(I16)[None] for c in cands]

        def count_tile(t, accs):
            r0 = pl.multiple_of(t * CNT_TILE, CNT_TILE)
            x = src_scr[pl.ds(r0, CNT_TILE), :].reshape(groups, PACK16, q_tile)
            out = []
            for acc, c16 in zip(accs, cands16):
                ge = jnp.where(x >= c16, one, zero)
                out.append(acc + _tree_sum([ge[r] for r in range(groups)]))
            return tuple(out)

        accs = lax.fori_loop(0, n_cnt_tiles, count_tile,
                             tuple(jnp.zeros((PACK16, q_tile), BF16) for _ in range(n_cand)))
        counts = [jnp.sum(a.astype(F32), axis=0, keepdims=True).astype(I32) for a in accs]
        new_tau, new_ge, new_gt = tau, c_ge, counts[0]
        for k in range(n_cand):
            ok = counts[k] >= k_target
            above = counts[k + 1] if k + 1 < n_cand else c_gt
            new_tau = jnp.where(ok, cands[k], new_tau)
            new_ge = jnp.where(ok, counts[k], new_ge)
            new_gt = jnp.where(ok, above, new_gt)
        return new_tau, new_ge, new_gt

    init = (jnp.full((1, q_tile), I16_MIN, I32), ge_all, jnp.zeros((1, q_tile), I32))
    return lax.fori_loop(0, 16 // bits, bit_pass, init)


def _attention_kernel(qi_ref, wit_ref, ki_ref, qlt_ref, c_ref, ct_ref, wuvt_ref, out_ref,
                      key_scr, khi_scr, klo_scr, ksel_scr, bias_scr,
                      sa_scr, sb_scr, pa_scr, pb_scr, aa_scr, ab_scr, m_scr, acc_scr, *, n_sel, idx_bits, q_tile, bits):
    j = pl.program_id(1)
    n_key_tiles = (j + 1) * (q_tile // KEY_TILE)
    n_cnt_tiles = (n_key_tiles + 3) // 4
    n_pairs = n_cnt_tiles

    q_pos = j * q_tile + lax.broadcasted_iota(I32, (KEY_TILE, q_tile), 1)
    k_off = lax.broadcasted_iota(I32, (KEY_TILE, q_tile), 0)

    def store_keys(r0, key):
        key_scr[pl.ds(r0, KEY_TILE), :] = key
        khi_scr[pl.ds(r0, KEY_TILE), :] = jnp.right_shift(key, 16).astype(I16)
        klo_scr[pl.ds(r0, KEY_TILE), :] = ((key & 0xFFFF) + I16_MIN).astype(I16)

    def score_tiles(i, carry):
        for sub in range(CNT_TILE // KEY_TILE):
            t = i * (CNT_TILE // KEY_TILE) + sub
            r0 = pl.multiple_of(t * KEY_TILE, KEY_TILE)
            ki_t = ki_ref[0, pl.ds(r0, KEY_TILE), :]
            score = jnp.zeros((KEY_TILE, q_tile), F32)
            for h in range(IDX_HEADS):
                logits = _dot_nt(ki_t, qi_ref[0, h])
                score = score + wit_ref[0, h:h + 1, :] * jnp.maximum(logits, 0.0)
            bits32 = pltpu.bitcast(score, I32)
            key = jnp.where(bits32 < 0, bits32 ^ jnp.int32(0x7FFFFFFF), bits32)
            key = jnp.where(score == 0.0, 0, key)
            store_keys(r0, jnp.where(t * KEY_TILE + k_off <= q_pos, key, INT_MIN))
        return carry

    lax.fori_loop(0, n_cnt_tiles, score_tiles, 0)

    k_sel = jnp.full((1, q_tile), n_sel, I32)
    n_rows = jnp.full((1, q_tile), n_cnt_tiles * CNT_TILE, I32)
    tau_hi, ge_hi, gt_hi = _radix_select16(khi_scr, n_cnt_tiles, k_sel, n_rows, q_tile, bits)
    tau_hi16 = jnp.broadcast_to(tau_hi, (PACK16, q_tile)).astype(I16)

    def sel_tile(t, carry):
        r0 = pl.multiple_of(t * CNT_TILE, CNT_TILE)
        hi = khi_scr[pl.ds(r0, CNT_TILE), :].reshape(CNT_TILE // PACK16, PACK16, q_tile)
        lo = klo_scr[pl.ds(r0, CNT_TILE), :].reshape(CNT_TILE // PACK16, PACK16, q_tile)
        sel = jnp.where(hi == tau_hi16[None], lo, jnp.full((), I16_MIN, I16))
        ksel_scr[pl.ds(r0, CNT_TILE), :] = sel.reshape(CNT_TILE, q_tile)
        return carry

    lax.fori_loop(0, n_cnt_tiles, sel_tile, 0)
    tau_lo, ge_lo, gt_lo = _radix_select16(ksel_scr, n_cnt_tiles, k_sel - gt_hi, ge_hi - gt_hi, q_tile, bits)
    tau = jnp.left_shift(tau_hi, 16) | (tau_lo - I16_MIN)
    need = n_sel - (gt_hi + gt_lo)
    excess = ((ge_lo - gt_lo) > need) & (tau != INT_MIN)

    @pl.when(jnp.max(jnp.where(excess, 1, 0)) > 0)
    def _():
        def idx_pass(i, lo_idx):
            cand = lo_idx + jnp.left_shift(jnp.int32(1), idx_bits - 1 - i)

            def count_tile(t, acc):
                r0 = pl.multiple_of(t * KEY_TILE, KEY_TILE)
                eq = jnp.where(key_scr[pl.ds(r0, KEY_TILE), :] == tau,
                               jnp.where(t * KEY_TILE + k_off < cand, 1, 0), 0)
                return acc + jnp.sum(eq.reshape(KEY_TILE // SUBLANES, SUBLANES, q_tile), axis=0)

            acc = lax.fori_loop(0, n_key_tiles, count_tile, jnp.zeros((SUBLANES, q_tile), I32))
            return jnp.where(jnp.sum(acc, axis=0, keepdims=True) < need, cand, lo_idx)

        cut = lax.fori_loop(0, idx_bits, idx_pass, jnp.zeros((1, q_tile), I32))

        def drop_tile(t, carry):
            r0 = pl.multiple_of(t * KEY_TILE, KEY_TILE)
            key = key_scr[pl.ds(r0, KEY_TILE), :]
            late = jnp.where(key == tau, jnp.where(t * KEY_TILE + k_off > cut, 1, 0), 0)
            key_scr[pl.ds(r0, KEY_TILE), :] = jnp.where(late == 1, INT_MIN, key)
            return carry

        lax.fori_loop(0, n_key_tiles, drop_tile, 0)

    ones_rows = jnp.ones((PACK16, KV_TILE), BF16)

    def scores(t, s_scr):
        r0 = pl.multiple_of(t * KV_TILE, KV_TILE)
        s_scr[...] = _dot(c_ref[0, pl.ds(r0, KV_TILE), :], qlt_ref[0, 0])

    def softmax(t, s_scr, p_scr, a_scr):
        for half in range(KV_TILE // KEY_TILE):
            kt = 2 * t + half
            r0 = pl.multiple_of(kt * KEY_TILE, KEY_TILE)
            bias_scr[half * KEY_TILE:(half + 1) * KEY_TILE, :] = jnp.where(
                key_scr[pl.ds(r0, KEY_TILE), :] >= tau,
                jnp.where(kt * KEY_TILE + k_off <= q_pos, 0.0, MASK_BIAS), MASK_BIAS).astype(F32)
        for h in range(ATT_HEADS):
            cols = slice(h * q_tile, (h + 1) * q_tile)
            s = s_scr[:, cols] + bias_scr[...]
            m_prev = m_scr[:, cols]
            m_new = jnp.maximum(m_prev, jnp.max(s, axis=0, keepdims=True))
            a_scr[:, cols] = jnp.exp2(m_prev - m_new)
            p_scr[:, cols] = jnp.exp2(s - m_new).astype(BF16)
            m_scr[:, cols] = m_new

    def values(t, p_scr, a_scr):
        c_aug_t = jnp.concatenate([ct_ref[0, t], ones_rows], axis=0)
        acc_scr[...] = a_scr[...] * acc_scr[...] + _dot(c_aug_t, p_scr[...])

    m_scr[...] = jnp.full(m_scr.shape, MASK_BIAS, F32)
    acc_scr[...] = jnp.zeros(acc_scr.shape, F32)
    pb_scr[...] = jnp.zeros(pb_scr.shape, BF16)
    ab_scr[...] = jnp.ones(ab_scr.shape, F32)
    last_tile = 2 * n_pairs - 1
    scores(0, sa_scr)

    def pair(u, carry):
        t0 = 2 * u
        scores(t0 + 1, sb_scr)
        softmax(t0, sa_scr, pa_scr, aa_scr)
        values(jnp.maximum(t0 - 1, 0), pb_scr, ab_scr)
        scores(jnp.minimum(t0 + 2, last_tile), sa_scr)
        softmax(t0 + 1, sb_scr, pb_scr, ab_scr)
        values(t0, pa_scr, aa_scr)
        return carry

    lax.fori_loop(0, n_pairs, pair, 0)
    values(last_tile, pb_scr, ab_scr)

    inv_l = 1.0 / acc_scr[KV_LATENT:KV_LATENT + 1, :]
    for h in range(ATT_HEADS):
        cols = slice(h * q_tile, (h + 1) * q_tile)
        o_lat_t = (acc_scr[:KV_LATENT, cols] * inv_l[:, cols]).astype(BF16)
        out_ref[0, :, h * ATT_HEAD_DIM:(h + 1) * ATT_HEAD_DIM] = _dot(wuvt_ref[h], o_lat_t).T


def _sparse_attention(qlt, qi, ki, wit, c, ct, wuvt, q_tile=Q_TILE, bits=1):
    bsz, seq, _ = c.shape
    assert seq % CNT_TILE == 0 and seq <= MAX_SEQ, seq
    nq = seq // q_tile
    n_sel = min(TOPK_MAX, seq // 4)
    cols = ATT_HEADS * q_tile
    kernel = functools.partial(_attention_kernel, n_sel=n_sel, idx_bits=max(1, (seq - 1).bit_length()),
                               q_tile=q_tile, bits=bits)
    return pl.pallas_call(
        kernel,
        grid=(bsz, nq),
        in_specs=[
            pl.BlockSpec((1, IDX_HEADS, q_tile, IDX_DIM), lambda b, j: (b, 0, j, 0)),
            pl.BlockSpec((1, IDX_HEADS, q_tile), lambda b, j: (b, 0, j)),
            pl.BlockSpec((1, seq, IDX_DIM), lambda b, j: (b, 0, 0)),
            pl.BlockSpec((1, 1, KV_LATENT, cols), lambda b, j: (b, j, 0, 0)),
            pl.BlockSpec((1, seq, KV_LATENT), lambda b, j: (b, 0, 0)),
            pl.BlockSpec((1, seq // KV_TILE, KV_LATENT, KV_TILE), lambda b, j: (b, 0, 0, 0)),
            pl.BlockSpec(wuvt.shape, lambda b, j: (0, 0, 0)),
        ],
        out_specs=pl.BlockSpec((1, q_tile, ATT_WIDTH), lambda b, j: (b, j, 0)),
        out_shape=jax.ShapeDtypeStruct((bsz, seq, ATT_WIDTH), F32),
        scratch_shapes=[
            pltpu.VMEM((seq, q_tile), I32),
            pltpu.VMEM((seq, q_tile), I16),
            pltpu.VMEM((seq, q_tile), I16),
            pltpu.VMEM((seq, q_tile), I16),
            pltpu.VMEM((KV_TILE, q_tile), F32),
            pltpu.VMEM((KV_TILE, cols), F32),
            pltpu.VMEM((KV_TILE, cols), F32),
            pltpu.VMEM((KV_TILE, cols), BF16),
            pltpu.VMEM((KV_TILE, cols), BF16),
            pltpu.VMEM((1, cols), F32),
            pltpu.VMEM((1, cols), F32),
            pltpu.VMEM((1, cols), F32),
            pltpu.VMEM((KV_LATENT + PACK16, cols), F32),
        ],
        compiler_params=pltpu.CompilerParams(
            dimension_semantics=("arbitrary", "arbitrary"), vmem_limit_bytes=VMEM_LIMIT_BYTES),
        name="sparse_attention",
    )(qi, wit, ki, qlt, c, ct, wuvt)


HALF_STATES = N_STATES // 2
HALF_COLS = 2 * HALF_STATES
SCAN_COLS = 256


def _s5_kernel(u_ref, gs_ref, bmat_ref, a_ref, cmat_ref, dskip_ref, wglu_ref, bglu_ref, out_ref,
               x_scr, state_scr):
    tile, bsz, _ = u_ref.shape
    rows = tile * bsz

    @pl.when(pl.program_id(0) == 0)
    def _():
        state_scr[...] = jnp.zeros(state_scr.shape, F32)

    u = u_ref[...].reshape(rows, SSM_WIDTH)
    ub = u.astype(BF16)
    half_in = SSM_WIDTH // 2
    x_scr[:, :HALF_COLS] = _dot(ub[:, :half_in], bmat_ref[0])
    x_scr[:, HALF_COLS:] = _dot(ub[:, half_in:], bmat_ref[1])

    for half in range(2):
        for off in range(0, HALF_STATES, SCAN_COLS):
            n0 = half * HALF_STATES + off
            cr = half * HALF_COLS + off
            ci = cr + HALF_STATES
            a_re = jnp.broadcast_to(a_ref[0:1, n0:n0 + SCAN_COLS], (bsz, SCAN_COLS))
            a_im = jnp.broadcast_to(a_ref[1:2, n0:n0 + SCAN_COLS], (bsz, SCAN_COLS))

            def step(t, carry, cr=cr, ci=ci, a_re=a_re, a_im=a_im):
                x_re, x_im = carry
                r0 = pl.multiple_of(t * bsz, bsz)
                n_re = a_re * x_re - a_im * x_im + x_scr[pl.ds(r0, bsz), cr:cr + SCAN_COLS]
                n_im = a_re * x_im + a_im * x_re + x_scr[pl.ds(r0, bsz), ci:ci + SCAN_COLS]
                x_scr[pl.ds(r0, bsz), cr:cr + SCAN_COLS] = n_re
                x_scr[pl.ds(r0, bsz), ci:ci + SCAN_COLS] = n_im
                return n_re, n_im

            x_re, x_im = lax.fori_loop(
                0, tile, step,
                (state_scr[:, cr:cr + SCAN_COLS], state_scr[:, ci:ci + SCAN_COLS]), unroll=4)
            state_scr[:, cr:cr + SCAN_COLS] = x_re
            state_scr[:, ci:ci + SCAN_COLS] = x_im

    y0 = _dot(x_scr[:, :HALF_COLS].astype(BF16), cmat_ref[0])
    y1 = _dot(x_scr[:, HALF_COLS:].astype(BF16), cmat_ref[1])
    y = jnp.concatenate([y0, y1], axis=1) + dskip_ref[...] * u
    y = _gelu_tanh(y)
    hcat = _dot(y.astype(BF16), wglu_ref[...]) + bglu_ref[...]
    ssm = hcat[:, :SSM_WIDTH] * _sigmoid(hcat[:, SSM_WIDTH:])
    gs = gs_ref[...].reshape(rows, SSM_WIDTH)
    out_ref[...] = (ssm * _silu(gs)).reshape(tile, bsz, SSM_WIDTH)


def _s5_branch(u_t, gs_t, bmat, a_bar, cmat, dskip, wglu, bglu, tile):
    seq, bsz, _ = u_t.shape
    const2 = lambda i: (0, 0)
    const3 = lambda i: (0, 0, 0)
    return pl.pallas_call(
        _s5_kernel,
        grid=(seq // tile,),
        in_specs=[
            pl.BlockSpec((tile, bsz, SSM_WIDTH), lambda i: (i, 0, 0)),
            pl.BlockSpec((tile, bsz, SSM_WIDTH), lambda i: (i, 0, 0)),
            pl.BlockSpec(bmat.shape, const3),
            pl.BlockSpec(a_bar.shape, const2),
            pl.BlockSpec(cmat.shape, const3),
            pl.BlockSpec(dskip.shape, const2),
            pl.BlockSpec(wglu.shape, const2),
            pl.BlockSpec(bglu.shape, const2),
        ],
        out_specs=pl.BlockSpec((tile, bsz, SSM_WIDTH), lambda i: (i, 0, 0)),
        out_shape=jax.ShapeDtypeStruct((seq, bsz, SSM_WIDTH), F32),
        scratch_shapes=[
            pltpu.VMEM((tile * bsz, 2 * N_STATES), F32),
            pltpu.VMEM((bsz, 2 * N_STATES), F32),
        ],
        compiler_params=pltpu.CompilerParams(
            dimension_semantics=("arbitrary",), vmem_limit_bytes=VMEM_LIMIT_BYTES),
        name="s5_branch",
    )(u_t, gs_t, bmat, a_bar, cmat, dskip, wglu, bglu)


def _s5_matrices(log_dt, a_re, a_im, b_re, b_im, c_re, c_im):
    dt = jnp.exp(log_dt.astype(F32))[:, None]
    lam_re, lam_im = a_re.astype(F32), a_im.astype(F32)
    mag = jnp.exp(lam_re * dt)
    ab_re = mag * jnp.cos(lam_im * dt)
    ab_im = mag * jnp.sin(lam_im * dt)
    den = lam_re * lam_re + lam_im * lam_im
    k_re = ((ab_re - 1.0) * lam_re + ab_im * lam_im) / den
    k_im = (ab_im * lam_re - (ab_re - 1.0) * lam_im) / den
    bb_re = k_re[..., None] * b_re - k_im[..., None] * b_im
    bb_im = k_re[..., None] * b_im + k_im[..., None] * b_re
    gh = N_GROUPS // 2
    eye = jnp.eye(gh, dtype=F32)

    def in_block(bb):
        return jnp.einsum('gpc,gh->gchp', bb, eye).reshape(gh * GROUP_CH, gh * STATE)

    def out_block(cc):
        return jnp.einsum('gcp,gh->gphc', cc, eye).reshape(gh * STATE, gh * GROUP_CH)

    bmat = jnp.stack([
        jnp.concatenate([in_block(bb_re[k * gh:(k + 1) * gh]), in_block(bb_im[k * gh:(k + 1) * gh])], axis=1)
        for k in range(2)])
    cmat = jnp.stack([
        jnp.concatenate([out_block(c_re[k * gh:(k + 1) * gh]), out_block(-c_im[k * gh:(k + 1) * gh])], axis=0)
        for k in range(2)])
    a_bar = jnp.stack([ab_re.reshape(N_STATES), ab_im.reshape(N_STATES)])
    return bmat.astype(BF16), a_bar, cmat.astype(BF16)


def _outproj_kernel(att_ref, ga_ref, ssm_ref, x_ref, wo_ref, lng_ref, lnb_ref, out_ref, *, alpha):
    att = att_ref[0] * _silu(ga_ref[0])
    y = _dot(att.astype(BF16), wo_ref[:ATT_WIDTH, :]) + _dot(ssm_ref[...].astype(BF16), wo_ref[ATT_WIDTH:, :])
    z = alpha * x_ref[0] + y
    mu = jnp.mean(z, axis=-1, keepdims=True)
    zc = z - mu
    var = jnp.mean(zc * zc, axis=-1, keepdims=True)
    out_ref[0] = zc * lax.rsqrt(var + LN_EPS) * lng_ref[...] + lnb_ref[...]


def _out_projection(att, ga, ssm_t, x, wo, lng, lnb, alpha, tile):
    bsz, seq, d = x.shape
    const2 = lambda b, i: (0, 0)
    return pl.pallas_call(
        functools.partial(_outproj_kernel, alpha=alpha),
        grid=(bsz, seq // tile),
        in_specs=[
            pl.BlockSpec((1, tile, ATT_WIDTH), lambda b, i: (b, i, 0)),
            pl.BlockSpec((1, tile, ATT_WIDTH), lambda b, i: (b, i, 0)),
            pl.BlockSpec((tile, SSM_WIDTH), lambda b, i: (i, b)),
            pl.BlockSpec((1, tile, d), lambda b, i: (b, i, 0)),
            pl.BlockSpec(wo.shape, const2),
            pl.BlockSpec(lng.shape, const2),
            pl.BlockSpec(lnb.shape, const2),
        ],
        out_specs=pl.BlockSpec((1, tile, d), lambda b, i: (b, i, 0)),
        out_shape=jax.ShapeDtypeStruct((bsz, seq, d), F32),
        compiler_params=pltpu.CompilerParams(
            dimension_semantics=("arbitrary", "arbitrary"), vmem_limit_bytes=VMEM_LIMIT_BYTES),
        name="out_projection",
    )(att, ga, ssm_t, x, wo, lng, lnb)


def _split_w_in(w_in):
    sizes = (ATT_WIDTH, KV_LATENT, IDX_HEADS * IDX_DIM, IDX_DIM, IDX_HEADS, ATT_WIDTH, SSM_WIDTH, SSM_WIDTH)
    parts, acc = [], 0
    for s in sizes:
        parts.append(w_in[:, acc:acc + s])
        acc += s
    wq, wckv, wqi, wki, wwi, wga, wu, wgs = parts
    pad = jnp.zeros((w_in.shape[0], 2 * LANES - KV_LATENT - IDX_DIM - IDX_HEADS), w_in.dtype)
    wsm = jnp.concatenate([wckv, wki, wwi, pad], axis=1)
    return [w.astype(BF16) for w in (wq, wsm, wqi, wga, wu, wgs)]


def _layer(x, alpha, cfg, w_in, kv_g, w_uk, w_uv, log_dt, a_re, a_im, b_re, b_im, c_re, c_im, d_skip,
           w_glu, b_glu, w_out, ln_g, ln_b):
    bsz, seq, d = x.shape
    row_tile = min(cfg[2], seq)
    scan_tile = min(cfg[3], seq)
    wq, wsm, wqi, wga, wu, wgs = _split_w_in(w_in)
    wukt = jnp.swapaxes(w_uk, 1, 2).astype(BF16)
    qlt, qi, ki, wit, c, ct, ga, u_t, gs_t = _in_projection(
        x, wq, wsm, wqi, wga, wu, wgs, wukt, kv_g.reshape(1, KV_LATENT).astype(F32), row_tile, cfg[0])
    att = _sparse_attention(qlt, qi, ki, wit, c, ct, jnp.swapaxes(w_uv, 1, 2).astype(BF16), cfg[0], cfg[1])
    bmat, a_bar, cmat = _s5_matrices(log_dt, a_re, a_im, b_re, b_im, c_re, c_im)
    ssm_t = _s5_branch(
        u_t.reshape(seq, bsz, SSM_WIDTH), gs_t.reshape(seq, bsz, SSM_WIDTH), bmat, a_bar, cmat,
        d_skip.reshape(1, SSM_WIDTH).astype(F32), w_glu.astype(BF16),
        b_glu.reshape(1, 2 * SSM_WIDTH).astype(F32), scan_tile)
    return _out_projection(
        att, ga, ssm_t.reshape(seq, bsz * SSM_WIDTH), x, w_out.astype(BF16),
        ln_g.reshape(1, d).astype(F32), ln_b.reshape(1, d).astype(F32), alpha, row_tile)


def kernel(x, w_in, kv_norm_g, w_uk, w_uv, log_dt, a_re, a_im, b_re, b_im, c_re, c_im, d_skip, w_glu, b_glu, w_out, ln_g, ln_b):
    depth = w_in.shape[0]
    alpha = (2 * depth) ** 0.25
    h = x
    for l in range(depth):
        h = _layer(h, alpha, (Q_TILE, RADIX_BITS, ROW_TILE, SCAN_TILE), w_in[l], kv_norm_g[l], w_uk[l], w_uv[l], log_dt[l], a_re[l], a_im[l],
                   b_re[l], b_im[l], c_re[l], c_im[l], d_skip[l], w_glu[l], b_glu[l],
                   w_out[l], ln_g[l], ln_b[l])
    return h
```

```python
import functools
import math

import jax
import jax.numpy as jnp
from jax import lax
from jax.experimental import pallas as pl
from jax.experimental.pallas import tpu as pltpu

ATT_HEADS = 8
ATT_HEAD_DIM = 64
ATT_WIDTH = ATT_HEADS * ATT_HEAD_DIM
KV_LATENT = 128
IDX_HEADS = 8
IDX_DIM = 32
TOPK_MAX = 256
SSM_WIDTH = 512
GROUP_CH = 16
N_GROUPS = SSM_WIDTH // GROUP_CH
STATE = 64
N_STATES = N_GROUPS * STATE
LN_EPS = 1e-5
RMS_EPS = 1e-6

LANES = 128
SUBLANES = 8
VMEM_LIMIT_BYTES = 56 * 1024 * 1024

Q_TILE = 256
RADIX_BITS = 1
ROW_TILE = 1024
SCAN_TILE = 64
KEY_TILE = 128
KV_TILE = 2 * KEY_TILE
INT_MIN = -(2 ** 31)
MASK_BIAS = -1e30
LOG2E = 1.4426950408889634

F32 = jnp.float32
BF16 = jnp.bfloat16
I32 = jnp.int32

_NT = (((1,), (1,)), ((), ()))


def _dot(a, b):
    return jnp.dot(a, b, preferred_element_type=F32)


def _dot_nt(a, b):
    return lax.dot_general(a, b, _NT, preferred_element_type=F32)


def _sigmoid(x):
    return 1.0 / (1.0 + jnp.exp(-x))


def _silu(x):
    return x * _sigmoid(x)


def _gelu_tanh(x):
    return 0.5 * x * (1.0 + jnp.tanh(math.sqrt(2.0 / math.pi) * (x + 0.044715 * (x * x * x))))


def _inproj_kernel(x_ref, wq_ref, wsm_ref, wqi_ref, wga_ref, wu_ref, wgs_ref, wukt_ref, kvg_ref,
                   qlt_ref, qi_ref, ki_ref, wit_ref, c_ref, ct_ref, ga_ref, u_ref, gs_ref, *, q_tile):
    x = x_ref[0].astype(BF16)
    q = _dot(x, wq_ref[...])
    q_scale = (ATT_HEAD_DIM ** -0.5) * LOG2E
    for h in range(ATT_HEADS):
        qh = q[:, h * ATT_HEAD_DIM:(h + 1) * ATT_HEAD_DIM].astype(BF16)
        ql_t = (_dot(qh, wukt_ref[h]) * q_scale).T.astype(BF16)
        for g in range(x.shape[0] // q_tile):
            qlt_ref[0, g, :, h * q_tile:(h + 1) * q_tile] = ql_t[:, g * q_tile:(g + 1) * q_tile]
    sm = _dot(x, wsm_ref[...])
    ckv = sm[:, :KV_LATENT]
    c = ckv * lax.rsqrt(jnp.mean(ckv * ckv, axis=-1, keepdims=True) + RMS_EPS) * kvg_ref[...]
    c_ref[0] = c.astype(BF16)
    c_t = c.T.astype(BF16)
    for g in range(x.shape[0] // KV_TILE):
        ct_ref[0, g] = c_t[:, g * KV_TILE:(g + 1) * KV_TILE]
    ki_ref[0] = sm[:, KV_LATENT:KV_LATENT + IDX_DIM].astype(BF16)
    sm_t = sm[:, KV_LATENT:].T
    wit_ref[0] = sm_t[IDX_DIM:IDX_DIM + IDX_HEADS, :] * (IDX_HEADS ** -0.5)
    qi = _dot(x, wqi_ref[...]) * (IDX_DIM ** -0.5)
    for h in range(IDX_HEADS):
        qi_ref[0, h] = qi[:, h * IDX_DIM:(h + 1) * IDX_DIM].astype(BF16)
    ga_ref[0] = _dot(x, wga_ref[...])
    u_ref[...] = _dot(x, wu_ref[...])
    gs_ref[...] = _dot(x, wgs_ref[...])


def _in_projection(x, wq, wsm, wqi, wga, wu, wgs, wukt, kvg, tile, q_tile):
    bsz, seq, d = x.shape
    nt = seq // tile
    const2 = lambda b, i: (0, 0)
    const3 = lambda b, i: (0, 0, 0)
    out_shape = (
        jax.ShapeDtypeStruct((bsz, seq // q_tile, KV_LATENT, ATT_HEADS * q_tile), BF16),
        jax.ShapeDtypeStruct((bsz, IDX_HEADS, seq, IDX_DIM), BF16),
        jax.ShapeDtypeStruct((bsz, seq, IDX_DIM), BF16),
        jax.ShapeDtypeStruct((bsz, IDX_HEADS, seq), F32),
        jax.ShapeDtypeStruct((bsz, seq, KV_LATENT), BF16),
        jax.ShapeDtypeStruct((bsz, seq // KV_TILE, KV_LATENT, KV_TILE), BF16),
        jax.ShapeDtypeStruct((bsz, seq, ATT_WIDTH), F32),
        jax.ShapeDtypeStruct((seq, bsz * SSM_WIDTH), F32),
        jax.ShapeDtypeStruct((seq, bsz * SSM_WIDTH), F32),
    )
    out_specs = (
        pl.BlockSpec((1, tile // q_tile, KV_LATENT, ATT_HEADS * q_tile), lambda b, i: (b, i, 0, 0)),
        pl.BlockSpec((1, IDX_HEADS, tile, IDX_DIM), lambda b, i: (b, 0, i, 0)),
        pl.BlockSpec((1, tile, IDX_DIM), lambda b, i: (b, i, 0)),
        pl.BlockSpec((1, IDX_HEADS, tile), lambda b, i: (b, 0, i)),
        pl.BlockSpec((1, tile, KV_LATENT), lambda b, i: (b, i, 0)),
        pl.BlockSpec((1, tile // KV_TILE, KV_LATENT, KV_TILE), lambda b, i: (b, i, 0, 0)),
        pl.BlockSpec((1, tile, ATT_WIDTH), lambda b, i: (b, i, 0)),
        pl.BlockSpec((tile, SSM_WIDTH), lambda b, i: (i, b)),
        pl.BlockSpec((tile, SSM_WIDTH), lambda b, i: (i, b)),
    )
    in_specs = [
        pl.BlockSpec((1, tile, d), lambda b, i: (b, i, 0)),
        pl.BlockSpec(wq.shape, const2),
        pl.BlockSpec(wsm.shape, const2),
        pl.BlockSpec(wqi.shape, const2),
        pl.BlockSpec(wga.shape, const2),
        pl.BlockSpec(wu.shape, const2),
        pl.BlockSpec(wgs.shape, const2),
        pl.BlockSpec(wukt.shape, const3),
        pl.BlockSpec(kvg.shape, const2),
    ]
    return pl.pallas_call(
        functools.partial(_inproj_kernel, q_tile=q_tile),
        grid=(bsz, nt),
        in_specs=in_specs,
        out_specs=out_specs,
        out_shape=out_shape,
        compiler_params=pltpu.CompilerParams(
            dimension_semantics=("arbitrary", "arbitrary"), vmem_limit_bytes=VMEM_LIMIT_BYTES),
        name="in_projection",
    )(x, wq, wsm, wqi, wga, wu, wgs, wukt, kvg)


CNT_TILE = 512
PACK16 = 16
I16 = jnp.int16
I16_MIN = -(2 ** 15)
MAX_SEQ = 256 * PACK16


def _tree_sum(xs):
    xs = list(xs)
    while len(xs) > 1:
        nxt = [xs[i] + xs[i + 1] for i in range(0, len(xs) - 1, 2)]
        if len(xs) % 2:
            nxt.append(xs[-1])
        xs = nxt
    return xs[0]


def _radix_select16(src_scr, n_cnt_tiles, k_target, ge_all, q_tile, bits):
    groups = CNT_TILE // PACK16
    n_cand = 2 ** bits - 1
    one, zero = jnp.ones((), BF16), jnp.zeros((), BF16)

    def bit_pass(i, state):
        tau, c_ge, c_gt = state
        step = jnp.left_shift(jnp.int32(1), 16 - bits * (i + 1))
        cands = [tau + k * step for k in range(1, n_cand + 1)]
        cands16 = [jnp.broadcast_to(c, (PACK16, q_tile)).astype(I16)[None] for c in cands]

        def count_tile(t, accs):
            r0 = pl.multiple_of(t * CNT_TILE, CNT_TILE)
            x = src_scr[pl.ds(r0, CNT_TILE), :].reshape(groups, PACK16, q_tile)
            out = []
            for acc, c16 in zip(accs, cands16):
                ge = jnp.where(x >= c16, one, zero)
                out.append(acc + _tree_sum([ge[r] for r in range(groups)]))
            return tuple(out)

        accs = lax.fori_loop(0, n_cnt_tiles, count_tile,
                             tuple(jnp.zeros((PACK16, q_tile), BF16) for _ in range(n_cand)))
        counts = [jnp.sum(a.astype(F32), axis=0, keepdims=True).astype(I32) for a in accs]
        new_tau, new_ge, new_gt = tau, c_ge, counts[0]
        for k in range(n_cand):
            ok = counts[k] >= k_target
            above = counts[k + 1] if k + 1 < n_cand else c_gt
            new_tau = jnp.where(ok, cands[k], new_tau)
            new_ge = jnp.where(ok, counts[k], new_ge)
            new_gt = jnp.where(ok, above, new_gt)
        return new_tau, new_ge, new_gt

    init = (jnp.full((1, q_tile), I16_MIN, I32), ge_all, jnp.zeros((1, q_tile), I32))
    return lax.fori_loop(0, 16 // bits, bit_pass, init)


def _attention_kernel(qi_ref, wit_ref, ki_ref, qlt_ref, c_ref, ct_ref, wuvt_ref, out_ref,
                      key_scr, khi_scr, klo_scr, ksel_scr, bias_scr,
                      sa_scr, sb_scr, pa_scr, pb_scr, aa_scr, ab_scr, m_scr, acc_scr, *, n_sel, idx_bits, q_tile, bits):
    j = pl.program_id(1)
    n_key_tiles = (j + 1) * (q_tile // KEY_TILE)
    n_cnt_tiles = (n_key_tiles + 3) // 4
    n_pairs = n_cnt_tiles

    q_pos = j * q_tile + lax.broadcasted_iota(I32, (KEY_TILE, q_tile), 1)
    k_off = lax.broadcasted_iota(I32, (KEY_TILE, q_tile), 0)

    def store_keys(r0, key):
        key_scr[pl.ds(r0, KEY_TILE), :] = key
        khi_scr[pl.ds(r0, KEY_TILE), :] = jnp.right_shift(key, 16).astype(I16)
        klo_scr[pl.ds(r0, KEY_TILE), :] = ((key & 0xFFFF) + I16_MIN).astype(I16)

    def score_tiles(i, carry):
        for sub in range(CNT_TILE // KEY_TILE):
            t = i * (CNT_TILE // KEY_TILE) + sub
            r0 = pl.multiple_of(t * KEY_TILE, KEY_TILE)
            ki_t = ki_ref[0, pl.ds(r0, KEY_TILE), :]
            score = jnp.zeros((KEY_TILE, q_tile), F32)
            for h in range(IDX_HEADS):
                logits = _dot_nt(ki_t, qi_ref[0, h])
                score = score + wit_ref[0, h:h + 1, :] * jnp.maximum(logits, 0.0)
            bits32 = pltpu.bitcast(score, I32)
            key = jnp.where(bits32 < 0, bits32 ^ jnp.int32(0x7FFFFFFF), bits32)
            key = jnp.where(score == 0.0, 0, key)
            store_keys(r0, jnp.where(t * KEY_TILE + k_off <= q_pos, key, INT_MIN))
        return carry

    lax.fori_loop(0, n_cnt_tiles, score_tiles, 0)

    k_sel = jnp.full((1, q_tile), n_sel, I32)
    n_rows = jnp.full((1, q_tile), n_cnt_tiles * CNT_TILE, I32)
    tau_hi, ge_hi, gt_hi = _radix_select16(khi_scr, n_cnt_tiles, k_sel, n_rows, q_tile, bits)
    tau_hi16 = jnp.broadcast_to(tau_hi, (PACK16, q_tile)).astype(I16)

    def sel_tile(t, carry):
        r0 = pl.multiple_of(t * CNT_TILE, CNT_TILE)
        hi = khi_scr[pl.ds(r0, CNT_TILE), :].reshape(CNT_TILE // PACK16, PACK16, q_tile)
        lo = klo_scr[pl.ds(r0, CNT_TILE), :].reshape(CNT_TILE // PACK16, PACK16, q_tile)
        sel = jnp.where(hi == tau_hi16[None], lo, jnp.full((), I16_MIN, I16))
        ksel_scr[pl.ds(r0, CNT_TILE), :] = sel.reshape(CNT_TILE, q_tile)
        return carry

    lax.fori_loop(0, n_cnt_tiles, sel_tile, 0)
    tau_lo, ge_lo, gt_lo = _radix_select16(ksel_scr, n_cnt_tiles, k_sel - gt_hi, ge_hi - gt_hi, q_tile, bits)
    tau = jnp.left_shift(tau_hi, 16) | (tau_lo - I16_MIN)
    need = n_sel - (gt_hi + gt_lo)
    excess = ((ge_lo - gt_lo) > need) & (tau != INT_MIN)

    @pl.when(jnp.max(jnp.where(excess, 1, 0)) > 0)
    def _():
        def idx_pass(i, lo_idx):
            cand = lo_idx + jnp.left_shift(jnp.int32(1), idx_bits - 1 - i)

            def count_tile(t, acc):
                r0 = pl.multiple_of(t * KEY_TILE, KEY_TILE)
                eq = jnp.where(key_scr[pl.ds(r0, KEY_TILE), :] == tau,
                               jnp.where(t * KEY_TILE + k_off < cand, 1, 0), 0)
                return acc + jnp.sum(eq.reshape(KEY_TILE // SUBLANES, SUBLANES, q_tile), axis=0)

            acc = lax.fori_loop(0, n_key_tiles, count_tile, jnp.zeros((SUBLANES, q_tile), I32))
            return jnp.where(jnp.sum(acc, axis=0, keepdims=True) < need, cand, lo_idx)

        cut = lax.fori_loop(0, idx_bits, idx_pass, jnp.zeros((1, q_tile), I32))

        def drop_tile(t, carry):
            r0 = pl.multiple_of(t * KEY_TILE, KEY_TILE)
            key = key_scr[pl.ds(r0, KEY_TILE), :]
            late = jnp.where(key == tau, jnp.where(t * KEY_TILE + k_off > cut, 1, 0), 0)
            key_scr[pl.ds(r0, KEY_TILE), :] = jnp.where(late == 1, INT_MIN, key)
            return carry

        lax.fori_loop(0, n_key_tiles, drop_tile, 0)

    ones_rows = jnp.ones((PACK16, KV_TILE), BF16)

    def scores(t, s_scr):
        r0 = pl.multiple_of(t * KV_TILE, KV_TILE)
        s_scr[...] = _dot(c_ref[0, pl.ds(r0, KV_TILE), :], qlt_ref[0, 0])

    def softmax(t, s_scr, p_scr, a_scr):
        for half in range(KV_TILE // KEY_TILE):
            kt = 2 * t + half
            r0 = pl.multiple_of(kt * KEY_TILE, KEY_TILE)
            bias_scr[half * KEY_TILE:(half + 1) * KEY_TILE, :] = jnp.where(
                key_scr[pl.ds(r0, KEY_TILE), :] >= tau,
                jnp.where(kt * KEY_TILE + k_off <= q_pos, 0.0, MASK_BIAS), MASK_BIAS).astype(F32)
        for h in range(ATT_HEADS):
            cols = slice(h * q_tile, (h + 1) * q_tile)
            s = (s_scr[:, cols] + bias_scr[...]).astype(BF16)
            m_prev = m_scr[:, cols]
            m_new = jnp.maximum(m_prev, jnp.max(s, axis=0, keepdims=True).astype(F32))
            a_scr[:, cols] = jnp.exp2(m_prev - m_new)
            p_scr[:, cols] = jnp.exp2(s - m_new.astype(BF16))
            m_scr[:, cols] = m_new

    def values(t, p_scr, a_scr):
        c_aug_t = jnp.concatenate([ct_ref[0, t], ones_rows], axis=0)
        acc_scr[...] = a_scr[...] * acc_scr[...] + _dot(c_aug_t, p_scr[...])

    m_scr[...] = jnp.full(m_scr.shape, MASK_BIAS, F32)
    acc_scr[...] = jnp.zeros(acc_scr.shape, F32)
    pb_scr[...] = jnp.zeros(pb_scr.shape, BF16)
    ab_scr[...] = jnp.ones(ab_scr.shape, F32)
    last_tile = 2 * n_pairs - 1
    scores(0, sa_scr)

    def pair(u, carry):
        t0 = 2 * u
        scores(t0 + 1, sb_scr)
        softmax(t0, sa_scr, pa_scr, aa_scr)
        values(jnp.maximum(t0 - 1, 0), pb_scr, ab_scr)
        scores(jnp.minimum(t0 + 2, last_tile), sa_scr)
        softmax(t0 + 1, sb_scr, pb_scr, ab_scr)
        values(t0, pa_scr, aa_scr)
        return carry

    lax.fori_loop(0, n_pairs, pair, 0)
    values(last_tile, pb_scr, ab_scr)

    inv_l = 1.0 / acc_scr[KV_LATENT:KV_LATENT + 1, :]
    for h in range(ATT_HEADS):
        cols = slice(h * q_tile, (h + 1) * q_tile)
        o_lat_t = (acc_scr[:KV_LATENT, cols] * inv_l[:, cols]).astype(BF16)
        out_ref[0, :, h * ATT_HEAD_DIM:(h + 1) * ATT_HEAD_DIM] = _dot(wuvt_ref[h], o_lat_t).T


def _sparse_attention(qlt, qi, ki, wit, c, ct, wuvt, q_tile=Q_TILE, bits=1):
    bsz, seq, _ = c.shape
    assert seq % CNT_TILE == 0 and seq <= MAX_SEQ, seq
    nq = seq // q_tile
    n_sel = min(TOPK_MAX, seq // 4)
    cols = ATT_HEADS * q_tile
    kernel = functools.partial(_attention_kernel, n_sel=n_sel, idx_bits=max(1, (seq - 1).bit_length()),
                               q_tile=q_tile, bits=bits)
    return pl.pallas_call(
        kernel,
        grid=(bsz, nq),
        in_specs=[
            pl.BlockSpec((1, IDX_HEADS, q_tile, IDX_DIM), lambda b, j: (b, 0, j, 0)),
            pl.BlockSpec((1, IDX_HEADS, q_tile), lambda b, j: (b, 0, j)),
            pl.BlockSpec((1, seq, IDX_DIM), lambda b, j: (b, 0, 0)),
            pl.BlockSpec((1, 1, KV_LATENT, cols), lambda b, j: (b, j, 0, 0)),
            pl.BlockSpec((1, seq, KV_LATENT), lambda b, j: (b, 0, 0)),
            pl.BlockSpec((1, seq // KV_TILE, KV_LATENT, KV_TILE), lambda b, j: (b, 0, 0, 0)),
            pl.BlockSpec(wuvt.shape, lambda b, j: (0, 0, 0)),
        ],
        out_specs=pl.BlockSpec((1, q_tile, ATT_WIDTH), lambda b, j: (b, j, 0)),
        out_shape=jax.ShapeDtypeStruct((bsz, seq, ATT_WIDTH), F32),
        scratch_shapes=[
            pltpu.VMEM((seq, q_tile), I32),
            pltpu.VMEM((seq, q_tile), I16),
            pltpu.VMEM((seq, q_tile), I16),
            pltpu.VMEM((seq, q_tile), I16),
            pltpu.VMEM((KV_TILE, q_tile), F32),
            pltpu.VMEM((KV_TILE, cols), F32),
            pltpu.VMEM((KV_TILE, cols), F32),
            pltpu.VMEM((KV_TILE, cols), BF16),
            pltpu.VMEM((KV_TILE, cols), BF16),
            pltpu.VMEM((1, cols), F32),
            pltpu.VMEM((1, cols), F32),
            pltpu.VMEM((1, cols), F32),
            pltpu.VMEM((KV_LATENT + PACK16, cols), F32),
        ],
        compiler_params=pltpu.CompilerParams(
            dimension_semantics=("arbitrary", "arbitrary"), vmem_limit_bytes=VMEM_LIMIT_BYTES),
        name="sparse_attention",
    )(qi, wit, ki, qlt, c, ct, wuvt)


HALF_STATES = N_STATES // 2
HALF_COLS = 2 * HALF_STATES
SCAN_COLS = 256


def _s5_kernel(u_ref, gs_ref, bmat_ref, a_ref, cmat_ref, dskip_ref, wglu_ref, bglu_ref, out_ref,
               x_scr, state_scr):
    tile, bsz, _ = u_ref.shape
    rows = tile * bsz

    @pl.when(pl.program_id(0) == 0)
    def _():
        state_scr[...] = jnp.zeros(state_scr.shape, F32)

    u = u_ref[...].reshape(rows, SSM_WIDTH)
    ub = u.astype(BF16)
    half_in = SSM_WIDTH // 2
    x_scr[:, :HALF_COLS] = _dot(ub[:, :half_in], bmat_ref[0])
    x_scr[:, HALF_COLS:] = _dot(ub[:, half_in:], bmat_ref[1])

    for half in range(2):
        for off in range(0, HALF_STATES, SCAN_COLS):
            n0 = half * HALF_STATES + off
            cr = half * HALF_COLS + off
            ci = cr + HALF_STATES
            a_re = jnp.broadcast_to(a_ref[0:1, n0:n0 + SCAN_COLS], (bsz, SCAN_COLS))
            a_im = jnp.broadcast_to(a_ref[1:2, n0:n0 + SCAN_COLS], (bsz, SCAN_COLS))

            def step(t, carry, cr=cr, ci=ci, a_re=a_re, a_im=a_im):
                x_re, x_im = carry
                r0 = pl.multiple_of(t * bsz, bsz)
                n_re = a_re * x_re - a_im * x_im + x_scr[pl.ds(r0, bsz), cr:cr + SCAN_COLS]
                n_im = a_re * x_im + a_im * x_re + x_scr[pl.ds(r0, bsz), ci:ci + SCAN_COLS]
                x_scr[pl.ds(r0, bsz), cr:cr + SCAN_COLS] = n_re
                x_scr[pl.ds(r0, bsz), ci:ci + SCAN_COLS] = n_im
                return n_re, n_im

            x_re, x_im = lax.fori_loop(
                0, tile, step,
                (state_scr[:, cr:cr + SCAN_COLS], state_scr[:, ci:ci + SCAN_COLS]), unroll=4)
            state_scr[:, cr:cr + SCAN_COLS] = x_re
            state_scr[:, ci:ci + SCAN_COLS] = x_im

    y0 = _dot(x_scr[:, :HALF_COLS].astype(BF16), cmat_ref[0])
    y1 = _dot(x_scr[:, HALF_COLS:].astype(BF16), cmat_ref[1])
    y = jnp.concatenate([y0, y1], axis=1) + dskip_ref[...] * u
    y = _gelu_tanh(y)
    hcat = _dot(y.astype(BF16), wglu_ref[...]) + bglu_ref[...]
    ssm = hcat[:, :SSM_WIDTH] * _sigmoid(hcat[:, SSM_WIDTH:])
    gs = gs_ref[...].reshape(rows, SSM_WIDTH)
    out_ref[...] = (ssm * _silu(gs)).astype(BF16).reshape(tile, bsz, SSM_WIDTH)


def _s5_branch(u_t, gs_t, bmat, a_bar, cmat, dskip, wglu, bglu, tile):
    seq, bsz, _ = u_t.shape
    const2 = lambda i: (0, 0)
    const3 = lambda i: (0, 0, 0)
    return pl.pallas_call(
        _s5_kernel,
        grid=(seq // tile,),
        in_specs=[
            pl.BlockSpec((tile, bsz, SSM_WIDTH), lambda i: (i, 0, 0)),
            pl.BlockSpec((tile, bsz, SSM_WIDTH), lambda i: (i, 0, 0)),
            pl.BlockSpec(bmat.shape, const3),
            pl.BlockSpec(a_bar.shape, const2),
            pl.BlockSpec(cmat.shape, const3),
            pl.BlockSpec(dskip.shape, const2),
            pl.BlockSpec(wglu.shape, const2),
            pl.BlockSpec(bglu.shape, const2),
        ],
        out_specs=pl.BlockSpec((tile, bsz, SSM_WIDTH), lambda i: (i, 0, 0)),
        out_shape=jax.ShapeDtypeStruct((seq, bsz, SSM_WIDTH), BF16),
        scratch_shapes=[
            pltpu.VMEM((tile * bsz, 2 * N_STATES), F32),
            pltpu.VMEM((bsz, 2 * N_STATES), F32),
        ],
        compiler_params=pltpu.CompilerParams(
            dimension_semantics=("arbitrary",), vmem_limit_bytes=VMEM_LIMIT_BYTES),
        name="s5_branch",
    )(u_t, gs_t, bmat, a_bar, cmat, dskip, wglu, bglu)


def _s5_matrices(log_dt, a_re, a_im, b_re, b_im, c_re, c_im):
    dt = jnp.exp(log_dt.astype(F32))[:, None]
    lam_re, lam_im = a_re.astype(F32), a_im.astype(F32)
    mag = jnp.exp(lam_re * dt)
    ab_re = mag * jnp.cos(lam_im * dt)
    ab_im = mag * jnp.sin(lam_im * dt)
    den = lam_re * lam_re + lam_im * lam_im
    k_re = ((ab_re - 1.0) * lam_re + ab_im * lam_im) / den
    k_im = (ab_im * lam_re - (ab_re - 1.0) * lam_im) / den
    bb_re = k_re[..., None] * b_re - k_im[..., None] * b_im
    bb_im = k_re[..., None] * b_im + k_im[..., None] * b_re
    gh = N_GROUPS // 2
    eye = jnp.eye(gh, dtype=F32)

    def in_block(bb):
        return jnp.einsum('gpc,gh->gchp', bb, eye).reshape(gh * GROUP_CH, gh * STATE)

    def out_block(cc):
        return jnp.einsum('gcp,gh->gphc', cc, eye).reshape(gh * STATE, gh * GROUP_CH)

    bmat = jnp.stack([
        jnp.concatenate([in_block(bb_re[k * gh:(k + 1) * gh]), in_block(bb_im[k * gh:(k + 1) * gh])], axis=1)
        for k in range(2)])
    cmat = jnp.stack([
        jnp.concatenate([out_block(c_re[k * gh:(k + 1) * gh]), out_block(-c_im[k * gh:(k + 1) * gh])], axis=0)
        for k in range(2)])
    a_bar = jnp.stack([ab_re.reshape(N_STATES), ab_im.reshape(N_STATES)])
    return bmat.astype(BF16), a_bar, cmat.astype(BF16)


def _outproj_kernel(att_ref, ga_ref, ssm_ref, x_ref, wo_ref, lng_ref, lnb_ref, out_ref, *, alpha):
    att = att_ref[0] * _silu(ga_ref[0])
    y = _dot(att.astype(BF16), wo_ref[:ATT_WIDTH, :]) + _dot(ssm_ref[...], wo_ref[ATT_WIDTH:, :])
    z = alpha * x_ref[0] + y
    mu = jnp.mean(z, axis=-1, keepdims=True)
    zc = z - mu
    var = jnp.mean(zc * zc, axis=-1, keepdims=True)
    out_ref[0] = zc * lax.rsqrt(var + LN_EPS) * lng_ref[...] + lnb_ref[...]


def _out_projection(att, ga, ssm_t, x, wo, lng, lnb, alpha, tile):
    bsz, seq, d = x.shape
    const2 = lambda b, i: (0, 0)
    return pl.pallas_call(
        functools.partial(_outproj_kernel, alpha=alpha),
        grid=(bsz, seq // tile),
        in_specs=[
            pl.BlockSpec((1, tile, ATT_WIDTH), lambda b, i: (b, i, 0)),
            pl.BlockSpec((1, tile, ATT_WIDTH), lambda b, i: (b, i, 0)),
            pl.BlockSpec((tile, SSM_WIDTH), lambda b, i: (i, b)),
            pl.BlockSpec((1, tile, d), lambda b, i: (b, i, 0)),
            pl.BlockSpec(wo.shape, const2),
            pl.BlockSpec(lng.shape, const2),
            pl.BlockSpec(lnb.shape, const2),
        ],
        out_specs=pl.BlockSpec((1, tile, d), lambda b, i: (b, i, 0)),
        out_shape=jax.ShapeDtypeStruct((bsz, seq, d), F32),
        compiler_params=pltpu.CompilerParams(
            dimension_semantics=("arbitrary", "arbitrary"), vmem_limit_bytes=VMEM_LIMIT_BYTES),
        name="out_projection",
    )(att, ga, ssm_t, x, wo, lng, lnb)


def _split_w_in(w_in):
    sizes = (ATT_WIDTH, KV_LATENT, IDX_HEADS * IDX_DIM, IDX_DIM, IDX_HEADS, ATT_WIDTH, SSM_WIDTH, SSM_WIDTH)
    parts, acc = [], 0
    for s in sizes:
        parts.append(w_in[:, acc:acc + s])
        acc += s
    wq, wckv, wqi, wki, wwi, wga, wu, wgs = parts
    pad = jnp.zeros((w_in.shape[0], 2 * LANES - KV_LATENT - IDX_DIM - IDX_HEADS), w_in.dtype)
    wsm = jnp.concatenate([wckv, wki, wwi, pad], axis=1)
    return [w.astype(BF16) for w in (wq, wsm, wqi, wga, wu, wgs)]


def _layer(x, alpha, cfg, w_in, kv_g, w_uk, w_uv, log_dt, a_re, a_im, b_re, b_im, c_re, c_im, d_skip,
           w_glu, b_glu, w_out, ln_g, ln_b):
    bsz, seq, d = x.shape
    row_tile = min(cfg[2], seq)
    scan_tile = min(cfg[3], seq)
    wq, wsm, wqi, wga, wu, wgs = _split_w_in(w_in)
    wukt = jnp.swapaxes(w_uk, 1, 2).astype(BF16)
    qlt, qi, ki, wit, c, ct, ga, u_t, gs_t = _in_projection(
        x, wq, wsm, wqi, wga, wu, wgs, wukt, kv_g.reshape(1, KV_LATENT).astype(F32), row_tile, cfg[0])
    att = _sparse_attention(qlt, qi, ki, wit, c, ct, jnp.swapaxes(w_uv, 1, 2).astype(BF16), cfg[0], cfg[1])
    bmat, a_bar, cmat = _s5_matrices(log_dt, a_re, a_im, b_re, b_im, c_re, c_im)
    ssm_t = _s5_branch(
        u_t.reshape(seq, bsz, SSM_WIDTH), gs_t.reshape(seq, bsz, SSM_WIDTH), bmat, a_bar, cmat,
        d_skip.reshape(1, SSM_WIDTH).astype(F32), w_glu.astype(BF16),
        b_glu.reshape(1, 2 * SSM_WIDTH).astype(F32), scan_tile)
    return _out_projection(
        att, ga, ssm_t.reshape(seq, bsz * SSM_WIDTH), x, w_out.astype(BF16),
        ln_g.reshape(1, d).astype(F32), ln_b.reshape(1, d).astype(F32), alpha, row_tile)


def kernel(x, w_in, kv_norm_g, w_uk, w_uv, log_dt, a_re, a_im, b_re, b_im, c_re, c_im, d_skip, w_glu, b_glu, w_out, ln_g, ln_b):
    depth = w_in.shape[0]
    alpha = (2 * depth) ** 0.25
    h = x
    for l in range(depth):
        h = _layer(h, alpha, (Q_TILE, RADIX_BITS, ROW_TILE, SCAN_TILE), w_in[l], kv_norm_g[l], w_uk[l], w_uv[l], log_dt[l], a_re[l], a_im[l],
                   b_re[l], b_im[l], c_re[l], c_im[l], d_skip[l], w_glu[l], b_glu[l],
                   w_out[l], ln_g[l], ln_b[l])
    return h
```

```python
import functools
import math

import jax
import jax.numpy as jnp
from jax import lax
from jax.experimental import pallas as pl
from jax.experimental.pallas import tpu as pltpu

ATT_HEADS = 8
ATT_HEAD_DIM = 64
ATT_WIDTH = ATT_HEADS * ATT_HEAD_DIM
KV_LATENT = 128
IDX_HEADS = 8
IDX_DIM = 32
TOPK_MAX = 256
SSM_WIDTH = 512
GROUP_CH = 16
N_GROUPS = SSM_WIDTH // GROUP_CH
STATE = 64
N_STATES = N_GROUPS * STATE
LN_EPS = 1e-5
RMS_EPS = 1e-6

LANES = 128
SUBLANES = 8
VMEM_LIMIT_BYTES = 56 * 1024 * 1024

Q_TILE = 256
RADIX_BITS = 1
ROW_TILE = 1024
SCAN_TILE = 64
KEY_TILE = 128
KV_TILE = 2 * KEY_TILE
INT_MIN = -(2 ** 31)
MASK_BIAS = -1e30
LOG2E = 1.4426950408889634

F32 = jnp.float32
BF16 = jnp.bfloat16
I32 = jnp.int32

_NT = (((1,), (1,)), ((), ()))


def _dot(a, b):
    return jnp.dot(a, b, preferred_element_type=F32)


def _dot_nt(a, b):
    return lax.dot_general(a, b, _NT, preferred_element_type=F32)


def _sigmoid(x):
    return 1.0 / (1.0 + jnp.exp(-x))


def _silu(x):
    return x * _sigmoid(x)


def _gelu_tanh(x):
    return 0.5 * x * (1.0 + jnp.tanh(math.sqrt(2.0 / math.pi) * (x + 0.044715 * (x * x * x))))


def _inproj_kernel(x_ref, wq_ref, wsm_ref, wqi_ref, wga_ref, wu_ref, wgs_ref, wukt_ref, kvg_ref,
                   qlt_ref, qi_ref, ki_ref, wit_ref, c_ref, ct_ref, ga_ref, u_ref, gs_ref, *, q_tile):
    x = x_ref[0].astype(BF16)
    q = _dot(x, wq_ref[...])
    q_scale = (ATT_HEAD_DIM ** -0.5) * LOG2E
    for h in range(ATT_HEADS):
        qh = q[:, h * ATT_HEAD_DIM:(h + 1) * ATT_HEAD_DIM].astype(BF16)
        ql_t = (_dot(qh, wukt_ref[h]) * q_scale).T.astype(BF16)
        for g in range(x.shape[0] // q_tile):
            qlt_ref[0, g, :, h * q_tile:(h + 1) * q_tile] = ql_t[:, g * q_tile:(g + 1) * q_tile]
    sm = _dot(x, wsm_ref[...])
    ckv = sm[:, :KV_LATENT]
    c = ckv * lax.rsqrt(jnp.mean(ckv * ckv, axis=-1, keepdims=True) + RMS_EPS) * kvg_ref[...]
    c_ref[0] = c.astype(BF16)
    c_t = c.T.astype(BF16)
    for g in range(x.shape[0] // KV_TILE):
        ct_ref[0, g] = c_t[:, g * KV_TILE:(g + 1) * KV_TILE]
    ki_ref[0] = sm[:, KV_LATENT:KV_LATENT + IDX_DIM].astype(BF16)
    sm_t = sm[:, KV_LATENT:].T
    wit_ref[0] = sm_t[IDX_DIM:IDX_DIM + IDX_HEADS, :] * (IDX_HEADS ** -0.5)
    qi = _dot(x, wqi_ref[...]) * (IDX_DIM ** -0.5)
    for h in range(IDX_HEADS):
        qi_ref[0, h] = qi[:, h * IDX_DIM:(h + 1) * IDX_DIM].astype(BF16)
    ga_ref[0] = _dot(x, wga_ref[...])
    u_ref[...] = _dot(x, wu_ref[...])
    gs_ref[...] = _dot(x, wgs_ref[...])


def _in_projection(x, wq, wsm, wqi, wga, wu, wgs, wukt, kvg, tile, q_tile):
    bsz, seq, d = x.shape
    nt = seq // tile
    const2 = lambda b, i: (0, 0)
    const3 = lambda b, i: (0, 0, 0)
    out_shape = (
        jax.ShapeDtypeStruct((bsz, seq // q_tile, KV_LATENT, ATT_HEADS * q_tile), BF16),
        jax.ShapeDtypeStruct((bsz, IDX_HEADS, seq, IDX_DIM), BF16),
        jax.ShapeDtypeStruct((bsz, seq, IDX_DIM), BF16),
        jax.ShapeDtypeStruct((bsz, IDX_HEADS, seq), F32),
        jax.ShapeDtypeStruct((bsz, seq, KV_LATENT), BF16),
        jax.ShapeDtypeStruct((bsz, seq // KV_TILE, KV_LATENT, KV_TILE), BF16),
        jax.ShapeDtypeStruct((bsz, seq, ATT_WIDTH), F32),
        jax.ShapeDtypeStruct((seq, bsz * SSM_WIDTH), F32),
        jax.ShapeDtypeStruct((seq, bsz * SSM_WIDTH), F32),
    )
    out_specs = (
        pl.BlockSpec((1, tile // q_tile, KV_LATENT, ATT_HEADS * q_tile), lambda b, i: (b, i, 0, 0)),
        pl.BlockSpec((1, IDX_HEADS, tile, IDX_DIM), lambda b, i: (b, 0, i, 0)),
        pl.BlockSpec((1, tile, IDX_DIM), lambda b, i: (b, i, 0)),
        pl.BlockSpec((1, IDX_HEADS, tile), lambda b, i: (b, 0, i)),
        pl.BlockSpec((1, tile, KV_LATENT), lambda b, i: (b, i, 0)),
        pl.BlockSpec((1, tile // KV_TILE, KV_LATENT, KV_TILE), lambda b, i: (b, i, 0, 0)),
        pl.BlockSpec((1, tile, ATT_WIDTH), lambda b, i: (b, i, 0)),
        pl.BlockSpec((tile, SSM_WIDTH), lambda b, i: (i, b)),
        pl.BlockSpec((tile, SSM_WIDTH), lambda b, i: (i, b)),
    )
    in_specs = [
        pl.BlockSpec((1, tile, d), lambda b, i: (b, i, 0)),
        pl.BlockSpec(wq.shape, const2),
        pl.BlockSpec(wsm.shape, const2),
        pl.BlockSpec(wqi.shape, const2),
        pl.BlockSpec(wga.shape, const2),
        pl.BlockSpec(wu.shape, const2),
        pl.BlockSpec(wgs.shape, const2),
        pl.BlockSpec(wukt.shape, const3),
        pl.BlockSpec(kvg.shape, const2),
    ]
    return pl.pallas_call(
        functools.partial(_inproj_kernel, q_tile=q_tile),
        grid=(bsz, nt),
        in_specs=in_specs,
        out_specs=out_specs,
        out_shape=out_shape,
        compiler_params=pltpu.CompilerParams(
            dimension_semantics=("arbitrary", "arbitrary"), vmem_limit_bytes=VMEM_LIMIT_BYTES),
        name="in_projection",
    )(x, wq, wsm, wqi, wga, wu, wgs, wukt, kvg)


CNT_TILE = 512
PACK16 = 16
I16 = jnp.int16
I16_MIN = -(2 ** 15)
MAX_SEQ = 256 * PACK16


def _tree_sum(xs):
    xs = list(xs)
    while len(xs) > 1:
        nxt = [xs[i] + xs[i + 1] for i in range(0, len(xs) - 1, 2)]
        if len(xs) % 2:
            nxt.append(xs[-1])
        xs = nxt
    return xs[0]


def _radix_select16(src_scr, n_cnt_tiles, k_target, ge_all, q_tile, bits):
    groups = CNT_TILE // PACK16
    n_cand = 2 ** bits - 1
    one, zero = jnp.ones((), BF16), jnp.zeros((), BF16)

    def bit_pass(i, state):
        tau, c_ge, c_gt = state
        step = jnp.left_shift(jnp.int32(1), 16 - bits * (i + 1))
        cands = [tau + k * step for k in range(1, n_cand + 1)]
        cands16 = [jnp.broadcast_to(c, (PACK16, q_tile)).astype(I16)[None] for c in cands]

        def count_tile(t, accs):
            r0 = pl.multiple_of(t * CNT_TILE, CNT_TILE)
            x = src_scr[pl.ds(r0, CNT_TILE), :].reshape(groups, PACK16, q_tile)
            out = []
            for acc, c16 in zip(accs, cands16):
                ge = jnp.where(x >= c16, one, zero)
                out.append(acc + _tree_sum([ge[r] for r in range(groups)]))
            return tuple(out)

        accs = lax.fori_loop(0, n_cnt_tiles, count_tile,
                             tuple(jnp.zeros((PACK16, q_tile), BF16) for _ in range(n_cand)))
        counts = [jnp.sum(a.astype(F32), axis=0, keepdims=True).astype(I32) for a in accs]
        new_tau, new_ge, new_gt = tau, c_ge, counts[0]
        for k in range(n_cand):
            ok = counts[k] >= k_target
            above = counts[k + 1] if k + 1 < n_cand else c_gt
            new_tau = jnp.where(ok, cands[k], new_tau)
            new_ge = jnp.where(ok, counts[k], new_ge)
            new_gt = jnp.where(ok, above, new_gt)
        return new_tau, new_ge, new_gt

    init = (jnp.full((1, q_tile), I16_MIN, I32), ge_all, jnp.zeros((1, q_tile), I32))
    return lax.fori_loop(0, 16 // bits, bit_pass, init)


def _attention_kernel(qi_ref, wit_ref, ki_ref, qlt_ref, c_ref, ct_ref, wuvt_ref, out_ref,
                      key_scr, khi_scr, klo_scr, ksel_scr, bias_scr,
                      sa_scr, sb_scr, pa_scr, pb_scr, aa_scr, ab_scr, m_scr, acc_scr, *, n_sel, idx_bits, q_tile, bits):
    j = pl.program_id(1)
    n_key_tiles = (j + 1) * (q_tile // KEY_TILE)
    n_cnt_tiles = (n_key_tiles + 3) // 4
    n_pairs = n_cnt_tiles

    q_pos = j * q_tile + lax.broadcasted_iota(I32, (KEY_TILE, q_tile), 1)
    k_off = lax.broadcasted_iota(I32, (KEY_TILE, q_tile), 0)

    def store_keys(r0, key):
        key_scr[pl.ds(r0, KEY_TILE), :] = key
        khi_scr[pl.ds(r0, KEY_TILE), :] = jnp.right_shift(key, 16).astype(I16)
        klo_scr[pl.ds(r0, KEY_TILE), :] = ((key & 0xFFFF) + I16_MIN).astype(I16)

    def score_tiles(i, carry):
        for sub in range(CNT_TILE // KEY_TILE):
            t = i * (CNT_TILE // KEY_TILE) + sub
            r0 = pl.multiple_of(t * KEY_TILE, KEY_TILE)
            ki_t = ki_ref[0, pl.ds(r0, KEY_TILE), :]
            score = jnp.zeros((KEY_TILE, q_tile), F32)
            for h in range(IDX_HEADS):
                logits = _dot_nt(ki_t, qi_ref[0, h])
                score = score + wit_ref[0, h:h + 1, :] * jnp.maximum(logits, 0.0)
            bits32 = pltpu.bitcast(score, I32)
            key = jnp.where(bits32 < 0, bits32 ^ jnp.int32(0x7FFFFFFF), bits32)
            key = jnp.where(score == 0.0, 0, key)
            store_keys(r0, jnp.where(t * KEY_TILE + k_off <= q_pos, key, INT_MIN))
        return carry

    lax.fori_loop(0, n_cnt_tiles, score_tiles, 0)

    k_sel = jnp.full((1, q_tile), n_sel, I32)
    n_rows = jnp.full((1, q_tile), n_cnt_tiles * CNT_TILE, I32)
    tau_hi, ge_hi, gt_hi = _radix_select16(khi_scr, n_cnt_tiles, k_sel, n_rows, q_tile, bits)
    tau_hi16 = jnp.broadcast_to(tau_hi, (PACK16, q_tile)).astype(I16)

    def sel_tile(t, carry):
        r0 = pl.multiple_of(t * CNT_TILE, CNT_TILE)
        hi = khi_scr[pl.ds(r0, CNT_TILE), :].reshape(CNT_TILE // PACK16, PACK16, q_tile)
        lo = klo_scr[pl.ds(r0, CNT_TILE), :].reshape(CNT_TILE // PACK16, PACK16, q_tile)
        sel = jnp.where(hi == tau_hi16[None], lo, jnp.full((), I16_MIN, I16))
        ksel_scr[pl.ds(r0, CNT_TILE), :] = sel.reshape(CNT_TILE, q_tile)
        return carry

    lax.fori_loop(0, n_cnt_tiles, sel_tile, 0)
    tau_lo, ge_lo, gt_lo = _radix_select16(ksel_scr, n_cnt_tiles, k_sel - gt_hi, ge_hi - gt_hi, q_tile, bits)
    tau = jnp.left_shift(tau_hi, 16) | (tau_lo - I16_MIN)
    need = n_sel - (gt_hi + gt_lo)
    excess = ((ge_lo - gt_lo) > need) & (tau != INT_MIN)

    @pl.when(jnp.max(jnp.where(excess, 1, 0)) > 0)
    def _():
        def idx_pass(i, lo_idx):
            cand = lo_idx + jnp.left_shift(jnp.int32(1), idx_bits - 1 - i)

            def count_tile(t, acc):
                r0 = pl.multiple_of(t * KEY_TILE, KEY_TILE)
                eq = jnp.where(key_scr[pl.ds(r0, KEY_TILE), :] == tau,
                               jnp.where(t * KEY_TILE + k_off < cand, 1, 0), 0)
                return acc + jnp.sum(eq.reshape(KEY_TILE // SUBLANES, SUBLANES, q_tile), axis=0)

            acc = lax.fori_loop(0, n_key_tiles, count_tile, jnp.zeros((SUBLANES, q_tile), I32))
            return jnp.where(jnp.sum(acc, axis=0, keepdims=True) < need, cand, lo_idx)

        cut = lax.fori_loop(0, idx_bits, idx_pass, jnp.zeros((1, q_tile), I32))

        def drop_tile(t, carry):
            r0 = pl.multiple_of(t * KEY_TILE, KEY_TILE)
            key = key_scr[pl.ds(r0, KEY_TILE), :]
            late = jnp.where(key == tau, jnp.where(t * KEY_TILE + k_off > cut, 1, 0), 0)
            key_scr[pl.ds(r0, KEY_TILE), :] = jnp.where(late == 1, INT_MIN, key)
            return carry

        lax.fori_loop(0, n_key_tiles, drop_tile, 0)

    ones_rows = jnp.ones((PACK16, KV_TILE), BF16)

    def scores(t, s_scr):
        r0 = pl.multiple_of(t * KV_TILE, KV_TILE)
        s_scr[...] = _dot(c_ref[0, pl.ds(r0, KV_TILE), :], qlt_ref[0, 0]).astype(BF16)

    def softmax(t, s_scr, p_scr, a_scr):
        for half in range(KV_TILE // KEY_TILE):
            kt = 2 * t + half
            r0 = pl.multiple_of(kt * KEY_TILE, KEY_TILE)
            bias_scr[half * KEY_TILE:(half + 1) * KEY_TILE, :] = jnp.where(
                key_scr[pl.ds(r0, KEY_TILE), :] >= tau,
                jnp.where(kt * KEY_TILE + k_off <= q_pos, 0.0, MASK_BIAS), MASK_BIAS).astype(BF16)
        for h in range(ATT_HEADS):
            cols = slice(h * q_tile, (h + 1) * q_tile)
            s = s_scr[:, cols] + bias_scr[...]
            m_prev = m_scr[:, cols]
            m_new = jnp.maximum(m_prev, jnp.max(s, axis=0, keepdims=True).astype(F32))
            a_scr[:, cols] = jnp.exp2(m_prev - m_new)
            p_scr[:, cols] = jnp.exp2(s - m_new.astype(BF16))
            m_scr[:, cols] = m_new

    def values(t, p_scr, a_scr):
        c_aug_t = jnp.concatenate([ct_ref[0, t], ones_rows], axis=0)
        acc_scr[...] = a_scr[...] * acc_scr[...] + _dot(c_aug_t, p_scr[...])

    m_scr[...] = jnp.full(m_scr.shape, MASK_BIAS, F32)
    acc_scr[...] = jnp.zeros(acc_scr.shape, F32)
    pb_scr[...] = jnp.zeros(pb_scr.shape, BF16)
    ab_scr[...] = jnp.ones(ab_scr.shape, F32)
    last_tile = 2 * n_pairs - 1
    scores(0, sa_scr)

    def pair(u, carry):
        t0 = 2 * u
        scores(t0 + 1, sb_scr)
        softmax(t0, sa_scr, pa_scr, aa_scr)
        values(jnp.maximum(t0 - 1, 0), pb_scr, ab_scr)
        scores(jnp.minimum(t0 + 2, last_tile), sa_scr)
        softmax(t0 + 1, sb_scr, pb_scr, ab_scr)
        values(t0, pa_scr, aa_scr)
        return carry

    lax.fori_loop(0, n_pairs, pair, 0)
    values(last_tile, pb_scr, ab_scr)

    inv_l = 1.0 / acc_scr[KV_LATENT:KV_LATENT + 1, :]
    for h in range(ATT_HEADS):
        cols = slice(h * q_tile, (h + 1) * q_tile)
        o_lat_t = (acc_scr[:KV_LATENT, cols] * inv_l[:, cols]).astype(BF16)
        out_ref[0, :, h * ATT_HEAD_DIM:(h + 1) * ATT_HEAD_DIM] = _dot(wuvt_ref[h], o_lat_t).T


def _sparse_attention(qlt, qi, ki, wit, c, ct, wuvt, q_tile=Q_TILE, bits=1):
    bsz, seq, _ = c.shape
    assert seq % CNT_TILE == 0 and seq <= MAX_SEQ, seq
    nq = seq // q_tile
    n_sel = min(TOPK_MAX, seq // 4)
    cols = ATT_HEADS * q_tile
    kernel = functools.partial(_attention_kernel, n_sel=n_sel, idx_bits=max(1, (seq - 1).bit_length()),
                               q_tile=q_tile, bits=bits)
    return pl.pallas_call(
        kernel,
        grid=(bsz, nq),
        in_specs=[
            pl.BlockSpec((1, IDX_HEADS, q_tile, IDX_DIM), lambda b, j: (b, 0, j, 0)),
            pl.BlockSpec((1, IDX_HEADS, q_tile), lambda b, j: (b, 0, j)),
            pl.BlockSpec((1, seq, IDX_DIM), lambda b, j: (b, 0, 0)),
            pl.BlockSpec((1, 1, KV_LATENT, cols), lambda b, j: (b, j, 0, 0)),
            pl.BlockSpec((1, seq, KV_LATENT), lambda b, j: (b, 0, 0)),
            pl.BlockSpec((1, seq // KV_TILE, KV_LATENT, KV_TILE), lambda b, j: (b, 0, 0, 0)),
            pl.BlockSpec(wuvt.shape, lambda b, j: (0, 0, 0)),
        ],
        out_specs=pl.BlockSpec((1, q_tile, ATT_WIDTH), lambda b, j: (b, j, 0)),
        out_shape=jax.ShapeDtypeStruct((bsz, seq, ATT_WIDTH), F32),
        scratch_shapes=[
            pltpu.VMEM((seq, q_tile), I32),
            pltpu.VMEM((seq, q_tile), I16),
            pltpu.VMEM((seq, q_tile), I16),
            pltpu.VMEM((seq, q_tile), I16),
            pltpu.VMEM((KV_TILE, q_tile), BF16),
            pltpu.VMEM((KV_TILE, cols), BF16),
            pltpu.VMEM((KV_TILE, cols), BF16),
            pltpu.VMEM((KV_TILE, cols), BF16),
            pltpu.VMEM((KV_TILE, cols), BF16),
            pltpu.VMEM((1, cols), F32),
            pltpu.VMEM((1, cols), F32),
            pltpu.VMEM((1, cols), F32),
            pltpu.VMEM((KV_LATENT + PACK16, cols), F32),
        ],
        compiler_params=pltpu.CompilerParams(
            dimension_semantics=("arbitrary", "arbitrary"), vmem_limit_bytes=VMEM_LIMIT_BYTES),
        name="sparse_attention",
    )(qi, wit, ki, qlt, c, ct, wuvt)


HALF_STATES = N_STATES // 2
HALF_COLS = 2 * HALF_STATES
SCAN_COLS = 256


def _s5_kernel(u_ref, gs_ref, bmat_ref, a_ref, cmat_ref, dskip_ref, wglu_ref, bglu_ref, out_ref,
               x_scr, state_scr):
    tile, bsz, _ = u_ref.shape
    rows = tile * bsz

    @pl.when(pl.program_id(0) == 0)
    def _():
        state_scr[...] = jnp.zeros(state_scr.shape, F32)

    u = u_ref[...].reshape(rows, SSM_WIDTH)
    ub = u.astype(BF16)
    half_in = SSM_WIDTH // 2
    x_scr[:, :HALF_COLS] = _dot(ub[:, :half_in], bmat_ref[0])
    x_scr[:, HALF_COLS:] = _dot(ub[:, half_in:], bmat_ref[1])

    for half in range(2):
        for off in range(0, HALF_STATES, SCAN_COLS):
            n0 = half * HALF_STATES + off
            cr = half * HALF_COLS + off
            ci = cr + HALF_STATES
            a_re = jnp.broadcast_to(a_ref[0:1, n0:n0 + SCAN_COLS], (bsz, SCAN_COLS))
            a_im = jnp.broadcast_to(a_ref[1:2, n0:n0 + SCAN_COLS], (bsz, SCAN_COLS))

            def step(t, carry, cr=cr, ci=ci, a_re=a_re, a_im=a_im):
                x_re, x_im = carry
                r0 = pl.multiple_of(t * bsz, bsz)
                n_re = a_re * x_re - a_im * x_im + x_scr[pl.ds(r0, bsz), cr:cr + SCAN_COLS]
                n_im = a_re * x_im + a_im * x_re + x_scr[pl.ds(r0, bsz), ci:ci + SCAN_COLS]
                x_scr[pl.ds(r0, bsz), cr:cr + SCAN_COLS] = n_re
                x_scr[pl.ds(r0, bsz), ci:ci + SCAN_COLS] = n_im
                return n_re, n_im

            x_re, x_im = lax.fori_loop(
                0, tile, step,
                (state_scr[:, cr:cr + SCAN_COLS], state_scr[:, ci:ci + SCAN_COLS]), unroll=4)
            state_scr[:, cr:cr + SCAN_COLS] = x_re
            state_scr[:, ci:ci + SCAN_COLS] = x_im

    y0 = _dot(x_scr[:, :HALF_COLS].astype(BF16), cmat_ref[0])
    y1 = _dot(x_scr[:, HALF_COLS:].astype(BF16), cmat_ref[1])
    y = jnp.concatenate([y0, y1], axis=1) + dskip_ref[...] * u
    y = _gelu_tanh(y)
    hcat = _dot(y.astype(BF16), wglu_ref[...]) + bglu_ref[...]
    ssm = hcat[:, :SSM_WIDTH] * _sigmoid(hcat[:, SSM_WIDTH:])
    gs = gs_ref[...].reshape(rows, SSM_WIDTH)
    out_ref[...] = (ssm * _silu(gs)).astype(BF16).reshape(tile, bsz, SSM_WIDTH)


def _s5_branch(u_t, gs_t, bmat, a_bar, cmat, dskip, wglu, bglu, tile):
    seq, bsz, _ = u_t.shape
    const2 = lambda i: (0, 0)
    const3 = lambda i: (0, 0, 0)
    return pl.pallas_call(
        _s5_kernel,
        grid=(seq // tile,),
        in_specs=[
            pl.BlockSpec((tile, bsz, SSM_WIDTH), lambda i: (i, 0, 0)),
            pl.BlockSpec((tile, bsz, SSM_WIDTH), lambda i: (i, 0, 0)),
            pl.BlockSpec(bmat.shape, const3),
            pl.BlockSpec(a_bar.shape, const2),
            pl.BlockSpec(cmat.shape, const3),
            pl.BlockSpec(dskip.shape, const2),
            pl.BlockSpec(wglu.shape, const2),
            pl.BlockSpec(bglu.shape, const2),
        ],
        out_specs=pl.BlockSpec((tile, bsz, SSM_WIDTH), lambda i: (i, 0, 0)),
        out_shape=jax.ShapeDtypeStruct((seq, bsz, SSM_WIDTH), BF16),
        scratch_shapes=[
            pltpu.VMEM((tile * bsz, 2 * N_STATES), F32),
            pltpu.VMEM((bsz, 2 * N_STATES), F32),
        ],
        compiler_params=pltpu.CompilerParams(
            dimension_semantics=("arbitrary",), vmem_limit_bytes=VMEM_LIMIT_BYTES),
        name="s5_branch",
    )(u_t, gs_t, bmat, a_bar, cmat, dskip, wglu, bglu)


def _s5_matrices(log_dt, a_re, a_im, b_re, b_im, c_re, c_im):
    dt = jnp.exp(log_dt.astype(F32))[:, None]
    lam_re, lam_im = a_re.astype(F32), a_im.astype(F32)
    mag = jnp.exp(lam_re * dt)
    ab_re = mag * jnp.cos(lam_im * dt)
    ab_im = mag * jnp.sin(lam_im * dt)
    den = lam_re * lam_re + lam_im * lam_im
    k_re = ((ab_re - 1.0) * lam_re + ab_im * lam_im) / den
    k_im = (ab_im * lam_re - (ab_re - 1.0) * lam_im) / den
    bb_re = k_re[..., None] * b_re - k_im[..., None] * b_im
    bb_im = k_re[..., None] * b_im + k_im[..., None] * b_re
    gh = N_GROUPS // 2
    eye = jnp.eye(gh, dtype=F32)

    def in_block(bb):
        return jnp.einsum('gpc,gh->gchp', bb, eye).reshape(gh * GROUP_CH, gh * STATE)

    def out_block(cc):
        return jnp.einsum('gcp,gh->gphc', cc, eye).reshape(gh * STATE, gh * GROUP_CH)

    bmat = jnp.stack([
        jnp.concatenate([in_block(bb_re[k * gh:(k + 1) * gh]), in_block(bb_im[k * gh:(k + 1) * gh])], axis=1)
        for k in range(2)])
    cmat = jnp.stack([
        jnp.concatenate([out_block(c_re[k * gh:(k + 1) * gh]), out_block(-c_im[k * gh:(k + 1) * gh])], axis=0)
        for k in range(2)])
    a_bar = jnp.stack([ab_re.reshape(N_STATES), ab_im.reshape(N_STATES)])
    return bmat.astype(BF16), a_bar, cmat.astype(BF16)


def _outproj_kernel(att_ref, ga_ref, ssm_ref, x_ref, wo_ref, lng_ref, lnb_ref, out_ref, *, alpha):
    att = att_ref[0] * _silu(ga_ref[0])
    y = _dot(att.astype(BF16), wo_ref[:ATT_WIDTH, :]) + _dot(ssm_ref[...], wo_ref[ATT_WIDTH:, :])
    z = alpha * x_ref[0] + y
    mu = jnp.mean(z, axis=-1, keepdims=True)
    zc = z - mu
    var = jnp.mean(zc * zc, axis=-1, keepdims=True)
    out_ref[0] = zc * lax.rsqrt(var + LN_EPS) * lng_ref[...] + lnb_ref[...]


def _out_projection(att, ga, ssm_t, x, wo, lng, lnb, alpha, tile):
    bsz, seq, d = x.shape
    const2 = lambda b, i: (0, 0)
    return pl.pallas_call(
        functools.partial(_outproj_kernel, alpha=alpha),
        grid=(bsz, seq // tile),
        in_specs=[
            pl.BlockSpec((1, tile, ATT_WIDTH), lambda b, i: (b, i, 0)),
            pl.BlockSpec((1, tile, ATT_WIDTH), lambda b, i: (b, i, 0)),
            pl.BlockSpec((tile, SSM_WIDTH), lambda b, i: (i, b)),
            pl.BlockSpec((1, tile, d), lambda b, i: (b, i, 0)),
            pl.BlockSpec(wo.shape, const2),
            pl.BlockSpec(lng.shape, const2),
            pl.BlockSpec(lnb.shape, const2),
        ],
        out_specs=pl.BlockSpec((1, tile, d), lambda b, i: (b, i, 0)),
        out_shape=jax.ShapeDtypeStruct((bsz, seq, d), F32),
        compiler_params=pltpu.CompilerParams(
            dimension_semantics=("arbitrary", "arbitrary"), vmem_limit_bytes=VMEM_LIMIT_BYTES),
        name="out_projection",
    )(att, ga, ssm_t, x, wo, lng, lnb)


def _split_w_in(w_in):
    sizes = (ATT_WIDTH, KV_LATENT, IDX_HEADS * IDX_DIM, IDX_DIM, IDX_HEADS, ATT_WIDTH, SSM_WIDTH, SSM_WIDTH)
    parts, acc = [], 0
    for s in sizes:
        parts.append(w_in[:, acc:acc + s])
        acc += s
    wq, wckv, wqi, wki, wwi, wga, wu, wgs = parts
    pad = jnp.zeros((w_in.shape[0], 2 * LANES - KV_LATENT - IDX_DIM - IDX_HEADS), w_in.dtype)
    wsm = jnp.concatenate([wckv, wki, wwi, pad], axis=1)
    return [w.astype(BF16) for w in (wq, wsm, wqi, wga, wu, wgs)]


def _layer(x, alpha, cfg, w_in, kv_g, w_uk, w_uv, log_dt, a_re, a_im, b_re, b_im, c_re, c_im, d_skip,
           w_glu, b_glu, w_out, ln_g, ln_b):
    bsz, seq, d = x.shape
    row_tile = min(cfg[2], seq)
    scan_tile = min(cfg[3], seq)
    wq, wsm, wqi, wga, wu, wgs = _split_w_in(w_in)
    wukt = jnp.swapaxes(w_uk, 1, 2).astype(BF16)
    qlt, qi, ki, wit, c, ct, ga, u_t, gs_t = _in_projection(
        x, wq, wsm, wqi, wga, wu, wgs, wukt, kv_g.reshape(1, KV_LATENT).astype(F32), row_tile, cfg[0])
    att = _sparse_attention(qlt, qi, ki, wit, c, ct, jnp.swapaxes(w_uv, 1, 2).astype(BF16), cfg[0], cfg[1])
    bmat, a_bar, cmat = _s5_matrices(log_dt, a_re, a_im, b_re, b_im, c_re, c_im)
    ssm_t = _s5_branch(
        u_t.reshape(seq, bsz, SSM_WIDTH), gs_t.reshape(seq, bsz, SSM_WIDTH), bmat, a_bar, cmat,
        d_skip.reshape(1, SSM_WIDTH).astype(F32), w_glu.astype(BF16),
        b_glu.reshape(1, 2 * SSM_WIDTH).astype(F32), scan_tile)
    return _out_projection(
        att, ga, ssm_t.reshape(seq, bsz * SSM_WIDTH), x, w_out.astype(BF16),
        ln_g.reshape(1, d).astype(F32), ln_b.reshape(1, d).astype(F32), alpha, row_tile)


def kernel(x, w_in, kv_norm_g, w_uk, w_uv, log_dt, a_re, a_im, b_re, b_im, c_re, c_im, d_skip, w_glu, b_glu, w_out, ln_g, ln_b):
    depth = w_in.shape[0]
    alpha = (2 * depth) ** 0.25
    h = x
    for l in range(depth):
        h = _layer(h, alpha, (Q_TILE, RADIX_BITS, ROW_TILE, SCAN_TILE), w_in[l], kv_norm_g[l], w_uk[l], w_uv[l], log_dt[l], a_re[l], a_im[l],
                   b_re[l], b_im[l], c_re[l], c_im[l], d_skip[l], w_glu[l], b_glu[l],
                   w_out[l], ln_g[l], ln_b[l])
    return h
```

```python
import functools
import math

import jax
import jax.numpy as jnp
from jax import lax
from jax.experimental import pallas as pl
from jax.experimental.pallas import tpu as pltpu

ATT_HEADS = 8
ATT_HEAD_DIM = 64
ATT_WIDTH = ATT_HEADS * ATT_HEAD_DIM
KV_LATENT = 128
IDX_HEADS = 8
IDX_DIM = 32
TOPK_MAX = 256
SSM_WIDTH = 512
GROUP_CH = 16
N_GROUPS = SSM_WIDTH // GROUP_CH
STATE = 64
N_STATES = N_GROUPS * STATE
LN_EPS = 1e-5
RMS_EPS = 1e-6

LANES = 128
SUBLANES = 8
VMEM_LIMIT_BYTES = 56 * 1024 * 1024

Q_TILE = 256
RADIX_BITS = 1
ROW_TILE = 1024
SCAN_TILE = 64
KEY_TILE = 128
KV_TILE = 2 * KEY_TILE
INT_MIN = -(2 ** 31)
MASK_BIAS = -1e30
LOG2E = 1.4426950408889634

F32 = jnp.float32
BF16 = jnp.bfloat16
I32 = jnp.int32

_NT = (((1,), (1,)), ((), ()))


def _dot(a, b):
    return jnp.dot(a, b, preferred_element_type=F32)


def _dot_nt(a, b):
    return lax.dot_general(a, b, _NT, preferred_element_type=F32)


def _sigmoid(x):
    return 1.0 / (1.0 + jnp.exp(-x))


def _silu(x):
    return x * _sigmoid(x)


def _gelu_tanh(x):
    return 0.5 * x * (1.0 + jnp.tanh(math.sqrt(2.0 / math.pi) * (x + 0.044715 * (x * x * x))))


def _inproj_kernel(x_ref, wq_ref, wsm_ref, wqi_ref, wga_ref, wu_ref, wgs_ref, wukt_ref, kvg_ref,
                   qlt_ref, qi_ref, ki_ref, wit_ref, c_ref, ct_ref, ga_ref, u_ref, gs_ref, *, q_tile):
    x = x_ref[0].astype(BF16)
    q = _dot(x, wq_ref[...])
    q_scale = (ATT_HEAD_DIM ** -0.5) * LOG2E
    for h in range(ATT_HEADS):
        qh = q[:, h * ATT_HEAD_DIM:(h + 1) * ATT_HEAD_DIM].astype(BF16)
        ql_t = (_dot(qh, wukt_ref[h]) * q_scale).T.astype(BF16)
        for g in range(x.shape[0] // q_tile):
            qlt_ref[0, g, :, h * q_tile:(h + 1) * q_tile] = ql_t[:, g * q_tile:(g + 1) * q_tile]
    sm = _dot(x, wsm_ref[...])
    ckv = sm[:, :KV_LATENT]
    c = ckv * lax.rsqrt(jnp.mean(ckv * ckv, axis=-1, keepdims=True) + RMS_EPS) * kvg_ref[...]
    c_ref[0] = c.astype(BF16)
    c_t = c.T.astype(BF16)
    for g in range(x.shape[0] // KV_TILE):
        ct_ref[0, g] = c_t[:, g * KV_TILE:(g + 1) * KV_TILE]
    ki_ref[0] = sm[:, KV_LATENT:KV_LATENT + IDX_DIM].astype(BF16)
    sm_t = sm[:, KV_LATENT:].T
    wit_ref[0] = sm_t[IDX_DIM:IDX_DIM + IDX_HEADS, :] * (IDX_HEADS ** -0.5)
    qi = _dot(x, wqi_ref[...]) * (IDX_DIM ** -0.5)
    for h in range(IDX_HEADS):
        qi_ref[0, h] = qi[:, h * IDX_DIM:(h + 1) * IDX_DIM].astype(BF16)
    ga_ref[0] = _dot(x, wga_ref[...])
    u_ref[...] = _dot(x, wu_ref[...])
    gs_ref[...] = _dot(x, wgs_ref[...])


def _in_projection(x, wq, wsm, wqi, wga, wu, wgs, wukt, kvg, tile, q_tile):
    bsz, seq, d = x.shape
    nt = seq // tile
    const2 = lambda b, i: (0, 0)
    const3 = lambda b, i: (0, 0, 0)
    out_shape = (
        jax.ShapeDtypeStruct((bsz, seq // q_tile, KV_LATENT, ATT_HEADS * q_tile), BF16),
        jax.ShapeDtypeStruct((bsz, IDX_HEADS, seq, IDX_DIM), BF16),
        jax.ShapeDtypeStruct((bsz, seq, IDX_DIM), BF16),
        jax.ShapeDtypeStruct((bsz, IDX_HEADS, seq), F32),
        jax.ShapeDtypeStruct((bsz, seq, KV_LATENT), BF16),
        jax.ShapeDtypeStruct((bsz, seq // KV_TILE, KV_LATENT, KV_TILE), BF16),
        jax.ShapeDtypeStruct((bsz, seq, ATT_WIDTH), F32),
        jax.ShapeDtypeStruct((seq, bsz * SSM_WIDTH), F32),
        jax.ShapeDtypeStruct((seq, bsz * SSM_WIDTH), F32),
    )
    out_specs = (
        pl.BlockSpec((1, tile // q_tile, KV_LATENT, ATT_HEADS * q_tile), lambda b, i: (b, i, 0, 0)),
        pl.BlockSpec((1, IDX_HEADS, tile, IDX_DIM), lambda b, i: (b, 0, i, 0)),
        pl.BlockSpec((1, tile, IDX_DIM), lambda b, i: (b, i, 0)),
        pl.BlockSpec((1, IDX_HEADS, tile), lambda b, i: (b, 0, i)),
        pl.BlockSpec((1, tile, KV_LATENT), lambda b, i: (b, i, 0)),
        pl.BlockSpec((1, tile // KV_TILE, KV_LATENT, KV_TILE), lambda b, i: (b, i, 0, 0)),
        pl.BlockSpec((1, tile, ATT_WIDTH), lambda b, i: (b, i, 0)),
        pl.BlockSpec((tile, SSM_WIDTH), lambda b, i: (i, b)),
        pl.BlockSpec((tile, SSM_WIDTH), lambda b, i: (i, b)),
    )
    in_specs = [
        pl.BlockSpec((1, tile, d), lambda b, i: (b, i, 0)),
        pl.BlockSpec(wq.shape, const2),
        pl.BlockSpec(wsm.shape, const2),
        pl.BlockSpec(wqi.shape, const2),
        pl.BlockSpec(wga.shape, const2),
        pl.BlockSpec(wu.shape, const2),
        pl.BlockSpec(wgs.shape, const2),
        pl.BlockSpec(wukt.shape, const3),
        pl.BlockSpec(kvg.shape, const2),
    ]
    return pl.pallas_call(
        functools.partial(_inproj_kernel, q_tile=q_tile),
        grid=(bsz, nt),
        in_specs=in_specs,
        out_specs=out_specs,
        out_shape=out_shape,
        compiler_params=pltpu.CompilerParams(
            dimension_semantics=("arbitrary", "arbitrary"), vmem_limit_bytes=VMEM_LIMIT_BYTES),
        name="in_projection",
    )(x, wq, wsm, wqi, wga, wu, wgs, wukt, kvg)


CNT_TILE = 512
PACK16 = 16
I16 = jnp.int16
I16_MIN = -(2 ** 15)
MAX_SEQ = 256 * PACK16


def _tree_sum(xs):
    xs = list(xs)
    while len(xs) > 1:
        nxt = [xs[i] + xs[i + 1] for i in range(0, len(xs) - 1, 2)]
        if len(xs) % 2:
            nxt.append(xs[-1])
        xs = nxt
    return xs[0]


def _radix_select16(src_scr, n_cnt_tiles, k_target, ge_all, q_tile, bits):
    groups = CNT_TILE // PACK16
    n_cand = 2 ** bits - 1
    one, zero = jnp.ones((), BF16), jnp.zeros((), BF16)

    def bit_pass(i, state):
        tau, c_ge, c_gt = state
        step = jnp.left_shift(jnp.int32(1), 16 - bits * (i + 1))
        cands = [tau + k * step for k in range(1, n_cand + 1)]
        cands16 = [jnp.broadcast_to(c, (PACK16, q_tile)).astype(I16)[None] for c in cands]

        def count_tile(t, accs):
            r0 = pl.multiple_of(t * CNT_TILE, CNT_TILE)
            x = src_scr[pl.ds(r0, CNT_TILE), :].reshape(groups, PACK16, q_tile)
            out = []
            for acc, c16 in zip(accs, cands16):
                ge = jnp.where(x >= c16, one, zero)
                out.append(acc + _tree_sum([ge[r] for r in range(groups)]))
            return tuple(out)

        accs = lax.fori_loop(0, n_cnt_tiles, count_tile,
                             tuple(jnp.zeros((PACK16, q_tile), BF16) for _ in range(n_cand)))
        counts = [jnp.sum(a.astype(F32), axis=0, keepdims=True).astype(I32) for a in accs]
        new_tau, new_ge, new_gt = tau, c_ge, counts[0]
        for k in range(n_cand):
            ok = counts[k] >= k_target
            above = counts[k + 1] if k + 1 < n_cand else c_gt
            new_tau = jnp.where(ok, cands[k], new_tau)
            new_ge = jnp.where(ok, counts[k], new_ge)
            new_gt = jnp.where(ok, above, new_gt)
        return new_tau, new_ge, new_gt

    init = (jnp.full((1, q_tile), I16_MIN, I32), ge_all, jnp.zeros((1, q_tile), I32))
    return lax.fori_loop(0, 16 // bits, bit_pass, init)


def _attention_kernel(qi_ref, wit_ref, ki_ref, qlt_ref, c_ref, ct_ref, wuvt_ref, out_ref,
                      key_scr, khi_scr, klo_scr, ksel_scr, bias_scr,
                      sa_scr, sb_scr, pa_scr, pb_scr, aa_scr, ab_scr, m_scr, acc_scr, *, n_sel, idx_bits, q_tile, bits):
    j = pl.program_id(1)
    n_key_tiles = (j + 1) * (q_tile // KEY_TILE)
    n_cnt_tiles = (n_key_tiles + 3) // 4
    n_pairs = n_cnt_tiles

    q_pos = j * q_tile + lax.broadcasted_iota(I32, (KEY_TILE, q_tile), 1)
    k_off = lax.broadcasted_iota(I32, (KEY_TILE, q_tile), 0)

    def store_keys(r0, key):
        key_scr[pl.ds(r0, KEY_TILE), :] = key
        khi_scr[pl.ds(r0, KEY_TILE), :] = jnp.right_shift(key, 16).astype(I16)
        klo_scr[pl.ds(r0, KEY_TILE), :] = ((key & 0xFFFF) + I16_MIN).astype(I16)

    def score_tiles(i, carry):
        for sub in range(CNT_TILE // KEY_TILE):
            t = i * (CNT_TILE // KEY_TILE) + sub
            r0 = pl.multiple_of(t * KEY_TILE, KEY_TILE)
            ki_t = ki_ref[0, pl.ds(r0, KEY_TILE), :]
            score = jnp.zeros((KEY_TILE, q_tile), F32)
            for h in range(IDX_HEADS):
                logits = _dot_nt(ki_t, qi_ref[0, h])
                score = score + wit_ref[0, h:h + 1, :] * jnp.maximum(logits, 0.0)
            bits32 = pltpu.bitcast(score, I32)
            key = jnp.where(bits32 < 0, bits32 ^ jnp.int32(0x7FFFFFFF), bits32)
            key = jnp.where(score == 0.0, 0, key)
            store_keys(r0, jnp.where(t * KEY_TILE + k_off <= q_pos, key, INT_MIN))
        return carry

    lax.fori_loop(0, n_cnt_tiles, score_tiles, 0)

    k_sel = jnp.full((1, q_tile), n_sel, I32)
    n_rows = jnp.full((1, q_tile), n_cnt_tiles * CNT_TILE, I32)
    tau_hi, ge_hi, gt_hi = _radix_select16(khi_scr, n_cnt_tiles, k_sel, n_rows, q_tile, bits)
    tau_hi16 = jnp.broadcast_to(tau_hi, (PACK16, q_tile)).astype(I16)

    def sel_tile(t, carry):
        r0 = pl.multiple_of(t * CNT_TILE, CNT_TILE)
        hi = khi_scr[pl.ds(r0, CNT_TILE), :].reshape(CNT_TILE // PACK16, PACK16, q_tile)
        lo = klo_scr[pl.ds(r0, CNT_TILE), :].reshape(CNT_TILE // PACK16, PACK16, q_tile)
        sel = jnp.where(hi == tau_hi16[None], lo, jnp.full((), I16_MIN, I16))
        ksel_scr[pl.ds(r0, CNT_TILE), :] = sel.reshape(CNT_TILE, q_tile)
        return carry

    lax.fori_loop(0, n_cnt_tiles, sel_tile, 0)
    tau_lo, ge_lo, gt_lo = _radix_select16(ksel_scr, n_cnt_tiles, k_sel - gt_hi, ge_hi - gt_hi, q_tile, bits)
    tau = jnp.left_shift(tau_hi, 16) | (tau_lo - I16_MIN)
    need = n_sel - (gt_hi + gt_lo)
    excess = ((ge_lo - gt_lo) > need) & (tau != INT_MIN)

    @pl.when(jnp.max(jnp.where(excess, 1, 0)) > 0)
    def _():
        def idx_pass(i, lo_idx):
            cand = lo_idx + jnp.left_shift(jnp.int32(1), idx_bits - 1 - i)

            def count_tile(t, acc):
                r0 = pl.multiple_of(t * KEY_TILE, KEY_TILE)
                eq = jnp.where(key_scr[pl.ds(r0, KEY_TILE), :] == tau,
                               jnp.where(t * KEY_TILE + k_off < cand, 1, 0), 0)
                return acc + jnp.sum(eq.reshape(KEY_TILE // SUBLANES, SUBLANES, q_tile), axis=0)

            acc = lax.fori_loop(0, n_key_tiles, count_tile, jnp.zeros((SUBLANES, q_tile), I32))
            return jnp.where(jnp.sum(acc, axis=0, keepdims=True) < need, cand, lo_idx)

        cut = lax.fori_loop(0, idx_bits, idx_pass, jnp.zeros((1, q_tile), I32))

        def drop_tile(t, carry):
            r0 = pl.multiple_of(t * KEY_TILE, KEY_TILE)
            key = key_scr[pl.ds(r0, KEY_TILE), :]
            late = jnp.where(key == tau, jnp.where(t * KEY_TILE + k_off > cut, 1, 0), 0)
            key_scr[pl.ds(r0, KEY_TILE), :] = jnp.where(late == 1, INT_MIN, key)
            return carry

        lax.fori_loop(0, n_key_tiles, drop_tile, 0)

    ones_rows = jnp.ones((PACK16, KV_TILE), BF16)

    def scores(t, s_scr):
        r0 = pl.multiple_of(t * KV_TILE, KV_TILE)
        s_scr[...] = _dot(c_ref[0, pl.ds(r0, KV_TILE), :], qlt_ref[0, 0]).astype(BF16)

    def softmax(t, s_scr, p_scr, a_scr):
        for half in range(KV_TILE // KEY_TILE):
            kt = 2 * t + half
            r0 = pl.multiple_of(kt * KEY_TILE, KEY_TILE)
            bias_scr[half * KEY_TILE:(half + 1) * KEY_TILE, :] = jnp.where(
                key_scr[pl.ds(r0, KEY_TILE), :] >= tau,
                jnp.where(kt * KEY_TILE + k_off <= q_pos, 0.0, MASK_BIAS), MASK_BIAS).astype(BF16)
        for h in range(ATT_HEADS):
            cols = slice(h * q_tile, (h + 1) * q_tile)
            s = s_scr[:, cols] + bias_scr[...]
            m_prev = m_scr[:, cols]
            m_new = jnp.maximum(m_prev, jnp.max(s, axis=0, keepdims=True).astype(F32))
            a_scr[:, cols] = jnp.exp2(m_prev - m_new)
            p_scr[:, cols] = jnp.exp2(s - m_new.astype(BF16))
            m_scr[:, cols] = m_new

    def values(t, p_scr, a_scr):
        c_aug_t = jnp.concatenate([ct_ref[0, t], ones_rows], axis=0)
        acc_scr[...] = a_scr[...] * acc_scr[...] + _dot(c_aug_t, p_scr[...])

    m_scr[...] = jnp.full(m_scr.shape, MASK_BIAS, F32)
    acc_scr[...] = jnp.zeros(acc_scr.shape, F32)

    def pair(u, carry):
        for t, s_scr, p_scr, a_scr in ((2 * u, sa_scr, pa_scr, aa_scr), (2 * u + 1, sb_scr, pb_scr, ab_scr)):
            scores(t, s_scr)
            softmax(t, s_scr, p_scr, a_scr)
            values(t, p_scr, a_scr)
        return carry

    lax.fori_loop(0, n_pairs, pair, 0)

    inv_l = 1.0 / acc_scr[KV_LATENT:KV_LATENT + 1, :]
    for h in range(ATT_HEADS):
        cols = slice(h * q_tile, (h + 1) * q_tile)
        o_lat_t = (acc_scr[:KV_LATENT, cols] * inv_l[:, cols]).astype(BF16)
        out_ref[0, :, h * ATT_HEAD_DIM:(h + 1) * ATT_HEAD_DIM] = _dot(wuvt_ref[h], o_lat_t).T


def _sparse_attention(qlt, qi, ki, wit, c, ct, wuvt, q_tile=Q_TILE, bits=1):
    bsz, seq, _ = c.shape
    assert seq % CNT_TILE == 0 and seq <= MAX_SEQ, seq
    nq = seq // q_tile
    n_sel = min(TOPK_MAX, seq // 4)
    cols = ATT_HEADS * q_tile
    kernel = functools.partial(_attention_kernel, n_sel=n_sel, idx_bits=max(1, (seq - 1).bit_length()),
                               q_tile=q_tile, bits=bits)
    return pl.pallas_call(
        kernel,
        grid=(bsz, nq),
        in_specs=[
            pl.BlockSpec((1, IDX_HEADS, q_tile, IDX_DIM), lambda b, j: (b, 0, j, 0)),
            pl.BlockSpec((1, IDX_HEADS, q_tile), lambda b, j: (b, 0, j)),
            pl.BlockSpec((1, seq, IDX_DIM), lambda b, j: (b, 0, 0)),
            pl.BlockSpec((1, 1, KV_LATENT, cols), lambda b, j: (b, j, 0, 0)),
            pl.BlockSpec((1, seq, KV_LATENT), lambda b, j: (b, 0, 0)),
            pl.BlockSpec((1, seq // KV_TILE, KV_LATENT, KV_TILE), lambda b, j: (b, 0, 0, 0)),
            pl.BlockSpec(wuvt.shape, lambda b, j: (0, 0, 0)),
        ],
        out_specs=pl.BlockSpec((1, q_tile, ATT_WIDTH), lambda b, j: (b, j, 0)),
        out_shape=jax.ShapeDtypeStruct((bsz, seq, ATT_WIDTH), F32),
        scratch_shapes=[
            pltpu.VMEM((seq, q_tile), I32),
            pltpu.VMEM((seq, q_tile), I16),
            pltpu.VMEM((seq, q_tile), I16),
            pltpu.VMEM((seq, q_tile), I16),
            pltpu.VMEM((KV_TILE, q_tile), BF16),
            pltpu.VMEM((KV_TILE, cols), BF16),
            pltpu.VMEM((KV_TILE, cols), BF16),
            pltpu.VMEM((KV_TILE, cols), BF16),
            pltpu.VMEM((KV_TILE, cols), BF16),
            pltpu.VMEM((1, cols), F32),
            pltpu.VMEM((1, cols), F32),
            pltpu.VMEM((1, cols), F32),
            pltpu.VMEM((KV_LATENT + PACK16, cols), F32),
        ],
        compiler_params=pltpu.CompilerParams(
            dimension_semantics=("arbitrary", "arbitrary"), vmem_limit_bytes=VMEM_LIMIT_BYTES),
        name="sparse_attention",
    )(qi, wit, ki, qlt, c, ct, wuvt)


HALF_STATES = N_STATES // 2
HALF_COLS = 2 * HALF_STATES
SCAN_COLS = 256


def _s5_kernel(u_ref, gs_ref, bmat_ref, a_ref, cmat_ref, dskip_ref, wglu_ref, bglu_ref, out_ref,
               x_scr, state_scr):
    tile, bsz, _ = u_ref.shape
    rows = tile * bsz

    @pl.when(pl.program_id(0) == 0)
    def _():
        state_scr[...] = jnp.zeros(state_scr.shape, F32)

    u = u_ref[...].reshape(rows, SSM_WIDTH)
    ub = u.astype(BF16)
    half_in = SSM_WIDTH // 2
    x_scr[:, :HALF_COLS] = _dot(ub[:, :half_in], bmat_ref[0])
    x_scr[:, HALF_COLS:] = _dot(ub[:, half_in:], bmat_ref[1])

    for half in range(2):
        for off in range(0, HALF_STATES, SCAN_COLS):
            n0 = half * HALF_STATES + off
            cr = half * HALF_COLS + off
            ci = cr + HALF_STATES
            a_re = jnp.broadcast_to(a_ref[0:1, n0:n0 + SCAN_COLS], (bsz, SCAN_COLS))
            a_im = jnp.broadcast_to(a_ref[1:2, n0:n0 + SCAN_COLS], (bsz, SCAN_COLS))

            def step(t, carry, cr=cr, ci=ci, a_re=a_re, a_im=a_im):
                x_re, x_im = carry
                r0 = pl.multiple_of(t * bsz, bsz)
                n_re = a_re * x_re - a_im * x_im + x_scr[pl.ds(r0, bsz), cr:cr + SCAN_COLS]
                n_im = a_re * x_im + a_im * x_re + x_scr[pl.ds(r0, bsz), ci:ci + SCAN_COLS]
                x_scr[pl.ds(r0, bsz), cr:cr + SCAN_COLS] = n_re
                x_scr[pl.ds(r0, bsz), ci:ci + SCAN_COLS] = n_im
                return n_re, n_im

            x_re, x_im = lax.fori_loop(
                0, tile, step,
                (state_scr[:, cr:cr + SCAN_COLS], state_scr[:, ci:ci + SCAN_COLS]), unroll=4)
            state_scr[:, cr:cr + SCAN_COLS] = x_re
            state_scr[:, ci:ci + SCAN_COLS] = x_im

    y0 = _dot(x_scr[:, :HALF_COLS].astype(BF16), cmat_ref[0])
    y1 = _dot(x_scr[:, HALF_COLS:].astype(BF16), cmat_ref[1])
    y = jnp.concatenate([y0, y1], axis=1) + dskip_ref[...] * u
    y = _gelu_tanh(y)
    hcat = _dot(y.astype(BF16), wglu_ref[...]) + bglu_ref[...]
    ssm = hcat[:, :SSM_WIDTH] * _sigmoid(hcat[:, SSM_WIDTH:])
    gs = gs_ref[...].reshape(rows, SSM_WIDTH)
    out_ref[...] = (ssm * _silu(gs)).astype(BF16).reshape(tile, bsz, SSM_WIDTH)


def _s5_branch(u_t, gs_t, bmat, a_bar, cmat, dskip, wglu, bglu, tile):
    seq, bsz, _ = u_t.shape
    const2 = lambda i: (0, 0)
    const3 = lambda i: (0, 0, 0)
    return pl.pallas_call(
        _s5_kernel,
        grid=(seq // tile,),
        in_specs=[
            pl.BlockSpec((tile, bsz, SSM_WIDTH), lambda i: (i, 0, 0)),
            pl.BlockSpec((tile, bsz, SSM_WIDTH), lambda i: (i, 0, 0)),
            pl.BlockSpec(bmat.shape, const3),
            pl.BlockSpec(a_bar.shape, const2),
            pl.BlockSpec(cmat.shape, const3),
            pl.BlockSpec(dskip.shape, const2),
            pl.BlockSpec(wglu.shape, const2),
            pl.BlockSpec(bglu.shape, const2),
        ],
        out_specs=pl.BlockSpec((tile, bsz, SSM_WIDTH), lambda i: (i, 0, 0)),
        out_shape=jax.ShapeDtypeStruct((seq, bsz, SSM_WIDTH), BF16),
        scratch_shapes=[
            pltpu.VMEM((tile * bsz, 2 * N_STATES), F32),
            pltpu.VMEM((bsz, 2 * N_STATES), F32),
        ],
        compiler_params=pltpu.CompilerParams(
            dimension_semantics=("arbitrary",), vmem_limit_bytes=VMEM_LIMIT_BYTES),
        name="s5_branch",
    )(u_t, gs_t, bmat, a_bar, cmat, dskip, wglu, bglu)


def _s5_matrices(log_dt, a_re, a_im, b_re, b_im, c_re, c_im):
    dt = jnp.exp(log_dt.astype(F32))[:, None]
    lam_re, lam_im = a_re.astype(F32), a_im.astype(F32)
    mag = jnp.exp(lam_re * dt)
    ab_re = mag * jnp.cos(lam_im * dt)
    ab_im = mag * jnp.sin(lam_im * dt)
    den = lam_re * lam_re + lam_im * lam_im
    k_re = ((ab_re - 1.0) * lam_re + ab_im * lam_im) / den
    k_im = (ab_im * lam_re - (ab_re - 1.0) * lam_im) / den
    bb_re = k_re[..., None] * b_re - k_im[..., None] * b_im
    bb_im = k_re[..., None] * b_im + k_im[..., None] * b_re
    gh = N_GROUPS // 2
    eye = jnp.eye(gh, dtype=F32)

    def in_block(bb):
        return jnp.einsum('gpc,gh->gchp', bb, eye).reshape(gh * GROUP_CH, gh * STATE)

    def out_block(cc):
        return jnp.einsum('gcp,gh->gphc', cc, eye).reshape(gh * STATE, gh * GROUP_CH)

    bmat = jnp.stack([
        jnp.concatenate([in_block(bb_re[k * gh:(k + 1) * gh]), in_block(bb_im[k * gh:(k + 1) * gh])], axis=1)
        for k in range(2)])
    cmat = jnp.stack([
        jnp.concatenate([out_block(c_re[k * gh:(k + 1) * gh]), out_block(-c_im[k * gh:(k + 1) * gh])], axis=0)
        for k in range(2)])
    a_bar = jnp.stack([ab_re.reshape(N_STATES), ab_im.reshape(N_STATES)])
    return bmat.astype(BF16), a_bar, cmat.astype(BF16)


def _outproj_kernel(att_ref, ga_ref, ssm_ref, x_ref, wo_ref, lng_ref, lnb_ref, out_ref, *, alpha):
    att = att_ref[0] * _silu(ga_ref[0])
    y = _dot(att.astype(BF16), wo_ref[:ATT_WIDTH, :]) + _dot(ssm_ref[...], wo_ref[ATT_WIDTH:, :])
    z = alpha * x_ref[0] + y
    mu = jnp.mean(z, axis=-1, keepdims=True)
    zc = z - mu
    var = jnp.mean(zc * zc, axis=-1, keepdims=True)
    out_ref[0] = zc * lax.rsqrt(var + LN_EPS) * lng_ref[...] + lnb_ref[...]


def _out_projection(att, ga, ssm_t, x, wo, lng, lnb, alpha, tile):
    bsz, seq, d = x.shape
    const2 = lambda b, i: (0, 0)
    return pl.pallas_call(
        functools.partial(_outproj_kernel, alpha=alpha),
        grid=(bsz, seq // tile),
        in_specs=[
            pl.BlockSpec((1, tile, ATT_WIDTH), lambda b, i: (b, i, 0)),
            pl.BlockSpec((1, tile, ATT_WIDTH), lambda b, i: (b, i, 0)),
            pl.BlockSpec((tile, SSM_WIDTH), lambda b, i: (i, b)),
            pl.BlockSpec((1, tile, d), lambda b, i: (b, i, 0)),
            pl.BlockSpec(wo.shape, const2),
            pl.BlockSpec(lng.shape, const2),
            pl.BlockSpec(lnb.shape, const2),
        ],
        out_specs=pl.BlockSpec((1, tile, d), lambda b, i: (b, i, 0)),
        out_shape=jax.ShapeDtypeStruct((bsz, seq, d), F32),
        compiler_params=pltpu.CompilerParams(
            dimension_semantics=("arbitrary", "arbitrary"), vmem_limit_bytes=VMEM_LIMIT_BYTES),
        name="out_projection",
    )(att, ga, ssm_t, x, wo, lng, lnb)


def _split_w_in(w_in):
    sizes = (ATT_WIDTH, KV_LATENT, IDX_HEADS * IDX_DIM, IDX_DIM, IDX_HEADS, ATT_WIDTH, SSM_WIDTH, SSM_WIDTH)
    parts, acc = [], 0
    for s in sizes:
        parts.append(w_in[:, acc:acc + s])
        acc += s
    wq, wckv, wqi, wki, wwi, wga, wu, wgs = parts
    pad = jnp.zeros((w_in.shape[0], 2 * LANES - KV_LATENT - IDX_DIM - IDX_HEADS), w_in.dtype)
    wsm = jnp.concatenate([wckv, wki, wwi, pad], axis=1)
    return [w.astype(BF16) for w in (wq, wsm, wqi, wga, wu, wgs)]


def _layer(x, alpha, cfg, w_in, kv_g, w_uk, w_uv, log_dt, a_re, a_im, b_re, b_im, c_re, c_im, d_skip,
           w_glu, b_glu, w_out, ln_g, ln_b):
    bsz, seq, d = x.shape
    row_tile = min(cfg[2], seq)
    scan_tile = min(cfg[3], seq)
    wq, wsm, wqi, wga, wu, wgs = _split_w_in(w_in)
    wukt = jnp.swapaxes(w_uk, 1, 2).astype(BF16)
    qlt, qi, ki, wit, c, ct, ga, u_t, gs_t = _in_projection(
        x, wq, wsm, wqi, wga, wu, wgs, wukt, kv_g.reshape(1, KV_LATENT).astype(F32), row_tile, cfg[0])
    att = _sparse_attention(qlt, qi, ki, wit, c, ct, jnp.swapaxes(w_uv, 1, 2).astype(BF16), cfg[0], cfg[1])
    bmat, a_bar, cmat = _s5_matrices(log_dt, a_re, a_im, b_re, b_im, c_re, c_im)
    ssm_t = _s5_branch(
        u_t.reshape(seq, bsz, SSM_WIDTH), gs_t.reshape(seq, bsz, SSM_WIDTH), bmat, a_bar, cmat,
        d_skip.reshape(1, SSM_WIDTH).astype(F32), w_glu.astype(BF16),
        b_glu.reshape(1, 2 * SSM_WIDTH).astype(F32), scan_tile)
    return _out_projection(
        att, ga, ssm_t.reshape(seq, bsz * SSM_WIDTH), x, w_out.astype(BF16),
        ln_g.reshape(1, d).astype(F32), ln_b.reshape(1, d).astype(F32), alpha, row_tile)


def kernel(x, w_in, kv_norm_g, w_uk, w_uv, log_dt, a_re, a_im, b_re, b_im, c_re, c_im, d_skip, w_glu, b_glu, w_out, ln_g, ln_b):
    depth = w_in.shape[0]
    alpha = (2 * depth) ** 0.25
    h = x
    for l in range(depth):
        h = _layer(h, alpha, (Q_TILE, RADIX_BITS, ROW_TILE, SCAN_TILE), w_in[l], kv_norm_g[l], w_uk[l], w_uv[l], log_dt[l], a_re[l], a_im[l],
                   b_re[l], b_im[l], c_re[l], c_im[l], d_skip[l], w_glu[l], b_glu[l],
                   w_out[l], ln_g[l], ln_b[l])
    return h
```

```python
import functools
import math

import jax
import jax.numpy as jnp
from jax import lax
from jax.experimental import pallas as pl
from jax.experimental.pallas import tpu as pltpu

ATT_HEADS = 8
ATT_HEAD_DIM = 64
ATT_WIDTH = ATT_HEADS * ATT_HEAD_DIM
KV_LATENT = 128
IDX_HEADS = 8
IDX_DIM = 32
TOPK_MAX = 256
SSM_WIDTH = 512
GROUP_CH = 16
N_GROUPS = SSM_WIDTH // GROUP_CH
STATE = 64
N_STATES = N_GROUPS * STATE
LN_EPS = 1e-5
RMS_EPS = 1e-6

LANES = 128
SUBLANES = 8
VMEM_LIMIT_BYTES = 56 * 1024 * 1024

Q_TILE = 256
RADIX_BITS = 1
ROW_TILE = 1024
SCAN_TILE = 64
KEY_TILE = 128
KV_TILE = 2 * KEY_TILE
INT_MIN = -(2 ** 31)
MASK_BIAS = -1e30
LOG2E = 1.4426950408889634

F32 = jnp.float32
BF16 = jnp.bfloat16
I32 = jnp.int32

_NT = (((1,), (1,)), ((), ()))


def _dot(a, b):
    return jnp.dot(a, b, preferred_element_type=F32)


def _dot_nt(a, b):
    return lax.dot_general(a, b, _NT, preferred_element_type=F32)


def _sigmoid(x):
    return 1.0 / (1.0 + jnp.exp(-x))


def _silu(x):
    return x * _sigmoid(x)


def _gelu_tanh(x):
    return 0.5 * x * (1.0 + jnp.tanh(math.sqrt(2.0 / math.pi) * (x + 0.044715 * (x * x * x))))


def _inproj_kernel(x_ref, wq_ref, wsm_ref, wqi_ref, wga_ref, wukt_ref, kvg_ref,
                   qlt_ref, qi_ref, ki_ref, wit_ref, c_ref, ct_ref, ga_ref, *, q_tile):
    x = x_ref[0].astype(BF16)
    q = _dot(x, wq_ref[...])
    q_scale = (ATT_HEAD_DIM ** -0.5) * LOG2E
    for h in range(ATT_HEADS):
        qh = q[:, h * ATT_HEAD_DIM:(h + 1) * ATT_HEAD_DIM].astype(BF16)
        ql_t = (_dot(qh, wukt_ref[h]) * q_scale).T.astype(BF16)
        for g in range(x.shape[0] // q_tile):
            qlt_ref[0, g, :, h * q_tile:(h + 1) * q_tile] = ql_t[:, g * q_tile:(g + 1) * q_tile]
    sm = _dot(x, wsm_ref[...])
    ckv = sm[:, :KV_LATENT]
    c = ckv * lax.rsqrt(jnp.mean(ckv * ckv, axis=-1, keepdims=True) + RMS_EPS) * kvg_ref[...]
    c_ref[0] = c.astype(BF16)
    c_t = c.T.astype(BF16)
    for g in range(x.shape[0] // KV_TILE):
        ct_ref[0, g] = c_t[:, g * KV_TILE:(g + 1) * KV_TILE]
    ki_ref[0] = sm[:, KV_LATENT:KV_LATENT + IDX_DIM].astype(BF16)
    sm_t = sm[:, KV_LATENT:].T
    wit_ref[0] = sm_t[IDX_DIM:IDX_DIM + IDX_HEADS, :] * (IDX_HEADS ** -0.5)
    qi = _dot(x, wqi_ref[...]) * (IDX_DIM ** -0.5)
    for h in range(IDX_HEADS):
        qi_ref[0, h] = qi[:, h * IDX_DIM:(h + 1) * IDX_DIM].astype(BF16)
    ga_ref[0] = _dot(x, wga_ref[...])


def _in_projection(x, wq, wsm, wqi, wga, wukt, kvg, tile, q_tile):
    bsz, seq, d = x.shape
    nt = seq // tile
    const2 = lambda b, i: (0, 0)
    const3 = lambda b, i: (0, 0, 0)
    out_shape = (
        jax.ShapeDtypeStruct((bsz, seq // q_tile, KV_LATENT, ATT_HEADS * q_tile), BF16),
        jax.ShapeDtypeStruct((bsz, IDX_HEADS, seq, IDX_DIM), BF16),
        jax.ShapeDtypeStruct((bsz, seq, IDX_DIM), BF16),
        jax.ShapeDtypeStruct((bsz, IDX_HEADS, seq), F32),
        jax.ShapeDtypeStruct((bsz, seq, KV_LATENT), BF16),
        jax.ShapeDtypeStruct((bsz, seq // KV_TILE, KV_LATENT, KV_TILE), BF16),
        jax.ShapeDtypeStruct((bsz, seq, ATT_WIDTH), F32),
    )
    out_specs = (
        pl.BlockSpec((1, tile // q_tile, KV_LATENT, ATT_HEADS * q_tile), lambda b, i: (b, i, 0, 0)),
        pl.BlockSpec((1, IDX_HEADS, tile, IDX_DIM), lambda b, i: (b, 0, i, 0)),
        pl.BlockSpec((1, tile, IDX_DIM), lambda b, i: (b, i, 0)),
        pl.BlockSpec((1, IDX_HEADS, tile), lambda b, i: (b, 0, i)),
        pl.BlockSpec((1, tile, KV_LATENT), lambda b, i: (b, i, 0)),
        pl.BlockSpec((1, tile // KV_TILE, KV_LATENT, KV_TILE), lambda b, i: (b, i, 0, 0)),
        pl.BlockSpec((1, tile, ATT_WIDTH), lambda b, i: (b, i, 0)),
    )
    in_specs = [
        pl.BlockSpec((1, tile, d), lambda b, i: (b, i, 0)),
        pl.BlockSpec(wq.shape, const2),
        pl.BlockSpec(wsm.shape, const2),
        pl.BlockSpec(wqi.shape, const2),
        pl.BlockSpec(wga.shape, const2),
        pl.BlockSpec(wukt.shape, const3),
        pl.BlockSpec(kvg.shape, const2),
    ]
    return pl.pallas_call(
        functools.partial(_inproj_kernel, q_tile=q_tile),
        grid=(bsz, nt),
        in_specs=in_specs,
        out_specs=out_specs,
        out_shape=out_shape,
        compiler_params=pltpu.CompilerParams(
            dimension_semantics=("arbitrary", "arbitrary"), vmem_limit_bytes=VMEM_LIMIT_BYTES),
        name="in_projection",
    )(x, wq, wsm, wqi, wga, wukt, kvg)


CNT_TILE = 512
PACK16 = 16
I16 = jnp.int16
I16_MIN = -(2 ** 15)
MAX_SEQ = 256 * PACK16


def _tree_sum(xs):
    xs = list(xs)
    while len(xs) > 1:
        nxt = [xs[i] + xs[i + 1] for i in range(0, len(xs) - 1, 2)]
        if len(xs) % 2:
            nxt.append(xs[-1])
        xs = nxt
    return xs[0]


def _radix_select16(src_scr, n_cnt_tiles, k_target, ge_all, q_tile, bits):
    groups = CNT_TILE // PACK16
    n_cand = 2 ** bits - 1
    one, zero = jnp.ones((), BF16), jnp.zeros((), BF16)

    def bit_pass(i, state):
        tau, c_ge, c_gt = state
        step = jnp.left_shift(jnp.int32(1), 16 - bits * (i + 1))
        cands = [tau + k * step for k in range(1, n_cand + 1)]
        cands16 = [jnp.broadcast_to(c, (PACK16, q_tile)).astype(I16)[None] for c in cands]

        def count_tile(t, accs):
            r0 = pl.multiple_of(t * CNT_TILE, CNT_TILE)
            x = src_scr[pl.ds(r0, CNT_TILE), :].reshape(groups, PACK16, q_tile)
            out = []
            for acc, c16 in zip(accs, cands16):
                ge = jnp.where(x >= c16, one, zero)
                out.append(acc + _tree_sum([ge[r] for r in range(groups)]))
            return tuple(out)

        accs = lax.fori_loop(0, n_cnt_tiles, count_tile,
                             tuple(jnp.zeros((PACK16, q_tile), BF16) for _ in range(n_cand)))
        counts = [jnp.sum(a.astype(F32), axis=0, keepdims=True).astype(I32) for a in accs]
        new_tau, new_ge, new_gt = tau, c_ge, counts[0]
        for k in range(n_cand):
            ok = counts[k] >= k_target
            above = counts[k + 1] if k + 1 < n_cand else c_gt
            new_tau = jnp.where(ok, cands[k], new_tau)
            new_ge = jnp.where(ok, counts[k], new_ge)
            new_gt = jnp.where(ok, above, new_gt)
        return new_tau, new_ge, new_gt

    init = (jnp.full((1, q_tile), I16_MIN, I32), ge_all, jnp.zeros((1, q_tile), I32))
    return lax.fori_loop(0, 16 // bits, bit_pass, init)


def _attention_kernel(qi_ref, wit_ref, ki_ref, qlt_ref, c_ref, ct_ref, wuvt_ref, out_ref,
                      key_scr, khi_scr, klo_scr, ksel_scr, bias_scr,
                      sa_scr, sb_scr, pa_scr, pb_scr, aa_scr, ab_scr, m_scr, acc_scr, *, n_sel, idx_bits, q_tile, bits):
    j = pl.program_id(1)
    n_key_tiles = (j + 1) * (q_tile // KEY_TILE)
    n_cnt_tiles = (n_key_tiles + 3) // 4
    n_pairs = n_cnt_tiles

    q_pos = j * q_tile + lax.broadcasted_iota(I32, (KEY_TILE, q_tile), 1)
    k_off = lax.broadcasted_iota(I32, (KEY_TILE, q_tile), 0)

    def store_keys(r0, key):
        key_scr[pl.ds(r0, KEY_TILE), :] = key
        khi_scr[pl.ds(r0, KEY_TILE), :] = jnp.right_shift(key, 16).astype(I16)
        klo_scr[pl.ds(r0, KEY_TILE), :] = ((key & 0xFFFF) + I16_MIN).astype(I16)

    def score_tiles(i, carry):
        for sub in range(CNT_TILE // KEY_TILE):
            t = i * (CNT_TILE // KEY_TILE) + sub
            r0 = pl.multiple_of(t * KEY_TILE, KEY_TILE)
            ki_t = ki_ref[0, pl.ds(r0, KEY_TILE), :]
            score = jnp.zeros((KEY_TILE, q_tile), F32)
            for h in range(IDX_HEADS):
                logits = _dot_nt(ki_t, qi_ref[0, h])
                score = score + wit_ref[0, h:h + 1, :] * jnp.maximum(logits, 0.0)
            bits32 = pltpu.bitcast(score, I32)
            key = jnp.where(bits32 < 0, bits32 ^ jnp.int32(0x7FFFFFFF), bits32)
            key = jnp.where(score == 0.0, 0, key)
            store_keys(r0, jnp.where(t * KEY_TILE + k_off <= q_pos, key, INT_MIN))
        return carry

    lax.fori_loop(0, n_cnt_tiles, score_tiles, 0)

    k_sel = jnp.full((1, q_tile), n_sel, I32)
    n_rows = jnp.full((1, q_tile), n_cnt_tiles * CNT_TILE, I32)
    tau_hi, ge_hi, gt_hi = _radix_select16(khi_scr, n_cnt_tiles, k_sel, n_rows, q_tile, bits)
    tau_hi16 = jnp.broadcast_to(tau_hi, (PACK16, q_tile)).astype(I16)

    def sel_tile(t, carry):
        r0 = pl.multiple_of(t * CNT_TILE, CNT_TILE)
        hi = khi_scr[pl.ds(r0, CNT_TILE), :].reshape(CNT_TILE // PACK16, PACK16, q_tile)
        lo = klo_scr[pl.ds(r0, CNT_TILE), :].reshape(CNT_TILE // PACK16, PACK16, q_tile)
        sel = jnp.where(hi == tau_hi16[None], lo, jnp.full((), I16_MIN, I16))
        ksel_scr[pl.ds(r0, CNT_TILE), :] = sel.reshape(CNT_TILE, q_tile)
        return carry

    lax.fori_loop(0, n_cnt_tiles, sel_tile, 0)
    tau_lo, ge_lo, gt_lo = _radix_select16(ksel_scr, n_cnt_tiles, k_sel - gt_hi, ge_hi - gt_hi, q_tile, bits)
    tau = jnp.left_shift(tau_hi, 16) | (tau_lo - I16_MIN)
    need = n_sel - (gt_hi + gt_lo)
    excess = ((ge_lo - gt_lo) > need) & (tau != INT_MIN)

    @pl.when(jnp.max(jnp.where(excess, 1, 0)) > 0)
    def _():
        def idx_pass(i, lo_idx):
            cand = lo_idx + jnp.left_shift(jnp.int32(1), idx_bits - 1 - i)

            def count_tile(t, acc):
                r0 = pl.multiple_of(t * KEY_TILE, KEY_TILE)
                eq = jnp.where(key_scr[pl.ds(r0, KEY_TILE), :] == tau,
                               jnp.where(t * KEY_TILE + k_off < cand, 1, 0), 0)
                return acc + jnp.sum(eq.reshape(KEY_TILE // SUBLANES, SUBLANES, q_tile), axis=0)

            acc = lax.fori_loop(0, n_key_tiles, count_tile, jnp.zeros((SUBLANES, q_tile), I32))
            return jnp.where(jnp.sum(acc, axis=0, keepdims=True) < need, cand, lo_idx)

        cut = lax.fori_loop(0, idx_bits, idx_pass, jnp.zeros((1, q_tile), I32))

        def drop_tile(t, carry):
            r0 = pl.multiple_of(t * KEY_TILE, KEY_TILE)
            key = key_scr[pl.ds(r0, KEY_TILE), :]
            late = jnp.where(key == tau, jnp.where(t * KEY_TILE + k_off > cut, 1, 0), 0)
            key_scr[pl.ds(r0, KEY_TILE), :] = jnp.where(late == 1, INT_MIN, key)
            return carry

        lax.fori_loop(0, n_key_tiles, drop_tile, 0)

    ones_rows = jnp.ones((PACK16, KV_TILE), BF16)

    def scores(t, s_scr):
        r0 = pl.multiple_of(t * KV_TILE, KV_TILE)
        s_scr[...] = _dot(c_ref[0, pl.ds(r0, KV_TILE), :], qlt_ref[0, 0]).astype(BF16)

    def softmax(t, s_scr, p_scr, a_scr):
        for half in range(KV_TILE // KEY_TILE):
            kt = 2 * t + half
            r0 = pl.multiple_of(kt * KEY_TILE, KEY_TILE)
            bias_scr[half * KEY_TILE:(half + 1) * KEY_TILE, :] = jnp.where(
                key_scr[pl.ds(r0, KEY_TILE), :] >= tau,
                jnp.where(kt * KEY_TILE + k_off <= q_pos, 0.0, MASK_BIAS), MASK_BIAS).astype(BF16)
        for h in range(ATT_HEADS):
            cols = slice(h * q_tile, (h + 1) * q_tile)
            s = s_scr[:, cols] + bias_scr[...]
            m_prev = m_scr[:, cols]
            m_new = jnp.maximum(m_prev, jnp.max(s, axis=0, keepdims=True).astype(F32))
            a_scr[:, cols] = jnp.exp2(m_prev - m_new)
            p_scr[:, cols] = jnp.exp2(s - m_new.astype(BF16))
            m_scr[:, cols] = m_new

    def values(t, p_scr, a_scr):
        c_aug_t = jnp.concatenate([ct_ref[0, t], ones_rows], axis=0)
        acc_scr[...] = a_scr[...] * acc_scr[...] + _dot(c_aug_t, p_scr[...])

    m_scr[...] = jnp.full(m_scr.shape, MASK_BIAS, F32)
    acc_scr[...] = jnp.zeros(acc_scr.shape, F32)

    def pair(u, carry):
        for t, s_scr, p_scr, a_scr in ((2 * u, sa_scr, pa_scr, aa_scr), (2 * u + 1, sb_scr, pb_scr, ab_scr)):
            scores(t, s_scr)
            softmax(t, s_scr, p_scr, a_scr)
            values(t, p_scr, a_scr)
        return carry

    lax.fori_loop(0, n_pairs, pair, 0)

    inv_l = 1.0 / acc_scr[KV_LATENT:KV_LATENT + 1, :]
    for h in range(ATT_HEADS):
        cols = slice(h * q_tile, (h + 1) * q_tile)
        o_lat_t = (acc_scr[:KV_LATENT, cols] * inv_l[:, cols]).astype(BF16)
        out_ref[0, :, h * ATT_HEAD_DIM:(h + 1) * ATT_HEAD_DIM] = _dot(wuvt_ref[h], o_lat_t).T


def _sparse_attention(qlt, qi, ki, wit, c, ct, wuvt, q_tile=Q_TILE, bits=1):
    bsz, seq, _ = c.shape
    assert seq % CNT_TILE == 0 and seq <= MAX_SEQ, seq
    nq = seq // q_tile
    n_sel = min(TOPK_MAX, seq // 4)
    cols = ATT_HEADS * q_tile
    kernel = functools.partial(_attention_kernel, n_sel=n_sel, idx_bits=max(1, (seq - 1).bit_length()),
                               q_tile=q_tile, bits=bits)
    return pl.pallas_call(
        kernel,
        grid=(bsz, nq),
        in_specs=[
            pl.BlockSpec((1, IDX_HEADS, q_tile, IDX_DIM), lambda b, j: (b, 0, j, 0)),
            pl.BlockSpec((1, IDX_HEADS, q_tile), lambda b, j: (b, 0, j)),
            pl.BlockSpec((1, seq, IDX_DIM), lambda b, j: (b, 0, 0)),
            pl.BlockSpec((1, 1, KV_LATENT, cols), lambda b, j: (b, j, 0, 0)),
            pl.BlockSpec((1, seq, KV_LATENT), lambda b, j: (b, 0, 0)),
            pl.BlockSpec((1, seq // KV_TILE, KV_LATENT, KV_TILE), lambda b, j: (b, 0, 0, 0)),
            pl.BlockSpec(wuvt.shape, lambda b, j: (0, 0, 0)),
        ],
        out_specs=pl.BlockSpec((1, q_tile, ATT_WIDTH), lambda b, j: (b, j, 0)),
        out_shape=jax.ShapeDtypeStruct((bsz, seq, ATT_WIDTH), F32),
        scratch_shapes=[
            pltpu.VMEM((seq, q_tile), I32),
            pltpu.VMEM((seq, q_tile), I16),
            pltpu.VMEM((seq, q_tile), I16),
            pltpu.VMEM((seq, q_tile), I16),
            pltpu.VMEM((KV_TILE, q_tile), BF16),
            pltpu.VMEM((KV_TILE, cols), BF16),
            pltpu.VMEM((KV_TILE, cols), BF16),
            pltpu.VMEM((KV_TILE, cols), BF16),
            pltpu.VMEM((KV_TILE, cols), BF16),
            pltpu.VMEM((1, cols), F32),
            pltpu.VMEM((1, cols), F32),
            pltpu.VMEM((1, cols), F32),
            pltpu.VMEM((KV_LATENT + PACK16, cols), F32),
        ],
        compiler_params=pltpu.CompilerParams(
            dimension_semantics=("arbitrary", "arbitrary"), vmem_limit_bytes=VMEM_LIMIT_BYTES),
        name="sparse_attention",
    )(qi, wit, ki, qlt, c, ct, wuvt)


HALF_STATES = N_STATES // 2
HALF_COLS = 2 * HALF_STATES
SCAN_COLS = 256


def _s5_kernel(xin_ref, wu_ref, wgs_ref, bmat_ref, a_ref, cmat_ref, dskip_ref, wglu_ref, bglu_ref, out_ref,
               x_scr, state_scr, u_scr, gs_scr):
    bsz, tile, d = xin_ref.shape
    rows = tile * bsz

    @pl.when(pl.program_id(0) == 0)
    def _():
        state_scr[...] = jnp.zeros(state_scr.shape, F32)

    xin = xin_ref[...].reshape(rows, d).astype(BF16)
    u_bt = _dot(xin, wu_ref[...])
    gs_bt = _dot(xin, wgs_ref[...])
    for b in range(bsz):
        u_scr[:, b, :] = u_bt[b * tile:(b + 1) * tile, :]
        gs_scr[:, b, :] = gs_bt[b * tile:(b + 1) * tile, :]
    u = u_scr[...].reshape(rows, SSM_WIDTH)
    ub = u.astype(BF16)
    half_in = SSM_WIDTH // 2
    x_scr[:, :HALF_COLS] = _dot(ub[:, :half_in], bmat_ref[0])
    x_scr[:, HALF_COLS:] = _dot(ub[:, half_in:], bmat_ref[1])

    for half in range(2):
        for off in range(0, HALF_STATES, SCAN_COLS):
            n0 = half * HALF_STATES + off
            cr = half * HALF_COLS + off
            ci = cr + HALF_STATES
            a_re = jnp.broadcast_to(a_ref[0:1, n0:n0 + SCAN_COLS], (bsz, SCAN_COLS))
            a_im = jnp.broadcast_to(a_ref[1:2, n0:n0 + SCAN_COLS], (bsz, SCAN_COLS))

            def step(t, carry, cr=cr, ci=ci, a_re=a_re, a_im=a_im):
                x_re, x_im = carry
                r0 = pl.multiple_of(t * bsz, bsz)
                n_re = a_re * x_re - a_im * x_im + x_scr[pl.ds(r0, bsz), cr:cr + SCAN_COLS]
                n_im = a_re * x_im + a_im * x_re + x_scr[pl.ds(r0, bsz), ci:ci + SCAN_COLS]
                x_scr[pl.ds(r0, bsz), cr:cr + SCAN_COLS] = n_re
                x_scr[pl.ds(r0, bsz), ci:ci + SCAN_COLS] = n_im
                return n_re, n_im

            x_re, x_im = lax.fori_loop(
                0, tile, step,
                (state_scr[:, cr:cr + SCAN_COLS], state_scr[:, ci:ci + SCAN_COLS]), unroll=4)
            state_scr[:, cr:cr + SCAN_COLS] = x_re
            state_scr[:, ci:ci + SCAN_COLS] = x_im

    y0 = _dot(x_scr[:, :HALF_COLS].astype(BF16), cmat_ref[0])
    y1 = _dot(x_scr[:, HALF_COLS:].astype(BF16), cmat_ref[1])
    y = jnp.concatenate([y0, y1], axis=1) + dskip_ref[...] * u
    y = _gelu_tanh(y)
    hcat = _dot(y.astype(BF16), wglu_ref[...]) + bglu_ref[...]
    ssm = hcat[:, :SSM_WIDTH] * _sigmoid(hcat[:, SSM_WIDTH:])
    gs = gs_scr[...].reshape(rows, SSM_WIDTH)
    out_ref[...] = (ssm * _silu(gs)).astype(BF16).reshape(tile, bsz, SSM_WIDTH)


def _s5_branch(x, wu, wgs, bmat, a_bar, cmat, dskip, wglu, bglu, tile):
    bsz, seq, d = x.shape
    const2 = lambda i: (0, 0)
    const3 = lambda i: (0, 0, 0)
    return pl.pallas_call(
        _s5_kernel,
        grid=(seq // tile,),
        in_specs=[
            pl.BlockSpec((bsz, tile, d), lambda i: (0, i, 0)),
            pl.BlockSpec(wu.shape, const2),
            pl.BlockSpec(wgs.shape, const2),
            pl.BlockSpec(bmat.shape, const3),
            pl.BlockSpec(a_bar.shape, const2),
            pl.BlockSpec(cmat.shape, const3),
            pl.BlockSpec(dskip.shape, const2),
            pl.BlockSpec(wglu.shape, const2),
            pl.BlockSpec(bglu.shape, const2),
        ],
        out_specs=pl.BlockSpec((tile, bsz, SSM_WIDTH), lambda i: (i, 0, 0)),
        out_shape=jax.ShapeDtypeStruct((seq, bsz, SSM_WIDTH), BF16),
        scratch_shapes=[
            pltpu.VMEM((tile * bsz, 2 * N_STATES), F32),
            pltpu.VMEM((bsz, 2 * N_STATES), F32),
            pltpu.VMEM((tile, bsz, SSM_WIDTH), F32),
            pltpu.VMEM((tile, bsz, SSM_WIDTH), F32),
        ],
        compiler_params=pltpu.CompilerParams(
            dimension_semantics=("arbitrary",), vmem_limit_bytes=VMEM_LIMIT_BYTES),
        name="s5_branch",
    )(x, wu, wgs, bmat, a_bar, cmat, dskip, wglu, bglu)


def _s5_matrices(log_dt, a_re, a_im, b_re, b_im, c_re, c_im):
    dt = jnp.exp(log_dt.astype(F32))[:, None]
    lam_re, lam_im = a_re.astype(F32), a_im.astype(F32)
    mag = jnp.exp(lam_re * dt)
    ab_re = mag * jnp.cos(lam_im * dt)
    ab_im = mag * jnp.sin(lam_im * dt)
    den = lam_re * lam_re + lam_im * lam_im
    k_re = ((ab_re - 1.0) * lam_re + ab_im * lam_im) / den
    k_im = (ab_im * lam_re - (ab_re - 1.0) * lam_im) / den
    bb_re = k_re[..., None] * b_re - k_im[..., None] * b_im
    bb_im = k_re[..., None] * b_im + k_im[..., None] * b_re
    gh = N_GROUPS // 2
    eye = jnp.eye(gh, dtype=F32)

    def in_block(bb):
        return jnp.einsum('gpc,gh->gchp', bb, eye).reshape(gh * GROUP_CH, gh * STATE)

    def out_block(cc):
        return jnp.einsum('gcp,gh->gphc', cc, eye).reshape(gh * STATE, gh * GROUP_CH)

    bmat = jnp.stack([
        jnp.concatenate([in_block(bb_re[k * gh:(k + 1) * gh]), in_block(bb_im[k * gh:(k + 1) * gh])], axis=1)
        for k in range(2)])
    cmat = jnp.stack([
        jnp.concatenate([out_block(c_re[k * gh:(k + 1) * gh]), out_block(-c_im[k * gh:(k + 1) * gh])], axis=0)
        for k in range(2)])
    a_bar = jnp.stack([ab_re.reshape(N_STATES), ab_im.reshape(N_STATES)])
    return bmat.astype(BF16), a_bar, cmat.astype(BF16)


def _outproj_kernel(att_ref, ga_ref, ssm_ref, x_ref, wo_ref, lng_ref, lnb_ref, out_ref, *, alpha):
    att = att_ref[0] * _silu(ga_ref[0])
    y = _dot(att.astype(BF16), wo_ref[:ATT_WIDTH, :]) + _dot(ssm_ref[...], wo_ref[ATT_WIDTH:, :])
    z = alpha * x_ref[0] + y
    mu = jnp.mean(z, axis=-1, keepdims=True)
    zc = z - mu
    var = jnp.mean(zc * zc, axis=-1, keepdims=True)
    out_ref[0] = zc * lax.rsqrt(var + LN_EPS) * lng_ref[...] + lnb_ref[...]


def _out_projection(att, ga, ssm_t, x, wo, lng, lnb, alpha, tile):
    bsz, seq, d = x.shape
    const2 = lambda b, i: (0, 0)
    return pl.pallas_call(
        functools.partial(_outproj_kernel, alpha=alpha),
        grid=(bsz, seq // tile),
        in_specs=[
            pl.BlockSpec((1, tile, ATT_WIDTH), lambda b, i: (b, i, 0)),
            pl.BlockSpec((1, tile, ATT_WIDTH), lambda b, i: (b, i, 0)),
            pl.BlockSpec((tile, SSM_WIDTH), lambda b, i: (i, b)),
            pl.BlockSpec((1, tile, d), lambda b, i: (b, i, 0)),
            pl.BlockSpec(wo.shape, const2),
            pl.BlockSpec(lng.shape, const2),
            pl.BlockSpec(lnb.shape, const2),
        ],
        out_specs=pl.BlockSpec((1, tile, d), lambda b, i: (b, i, 0)),
        out_shape=jax.ShapeDtypeStruct((bsz, seq, d), F32),
        compiler_params=pltpu.CompilerParams(
            dimension_semantics=("arbitrary", "arbitrary"), vmem_limit_bytes=VMEM_LIMIT_BYTES),
        name="out_projection",
    )(att, ga, ssm_t, x, wo, lng, lnb)


def _split_w_in(w_in):
    sizes = (ATT_WIDTH, KV_LATENT, IDX_HEADS * IDX_DIM, IDX_DIM, IDX_HEADS, ATT_WIDTH, SSM_WIDTH, SSM_WIDTH)
    parts, acc = [], 0
    for s in sizes:
        parts.append(w_in[:, acc:acc + s])
        acc += s
    wq, wckv, wqi, wki, wwi, wga, wu, wgs = parts
    pad = jnp.zeros((w_in.shape[0], 2 * LANES - KV_LATENT - IDX_DIM - IDX_HEADS), w_in.dtype)
    wsm = jnp.concatenate([wckv, wki, wwi, pad], axis=1)
    return [w.astype(BF16) for w in (wq, wsm, wqi, wga, wu, wgs)]


def _layer(x, alpha, cfg, w_in, kv_g, w_uk, w_uv, log_dt, a_re, a_im, b_re, b_im, c_re, c_im, d_skip,
           w_glu, b_glu, w_out, ln_g, ln_b):
    bsz, seq, d = x.shape
    row_tile = min(cfg[2], seq)
    scan_tile = min(cfg[3], seq)
    wq, wsm, wqi, wga, wu, wgs = _split_w_in(w_in)
    wukt = jnp.swapaxes(w_uk, 1, 2).astype(BF16)
    qlt, qi, ki, wit, c, ct, ga = _in_projection(
        x, wq, wsm, wqi, wga, wukt, kv_g.reshape(1, KV_LATENT).astype(F32), row_tile, cfg[0])
    att = _sparse_attention(qlt, qi, ki, wit, c, ct, jnp.swapaxes(w_uv, 1, 2).astype(BF16), cfg[0], cfg[1])
    bmat, a_bar, cmat = _s5_matrices(log_dt, a_re, a_im, b_re, b_im, c_re, c_im)
    ssm_t = _s5_branch(
        x, wu, wgs, bmat, a_bar, cmat,
        d_skip.reshape(1, SSM_WIDTH).astype(F32), w_glu.astype(BF16),
        b_glu.reshape(1, 2 * SSM_WIDTH).astype(F32), scan_tile)
    return _out_projection(
        att, ga, ssm_t.reshape(seq, bsz * SSM_WIDTH), x, w_out.astype(BF16),
        ln_g.reshape(1, d).astype(F32), ln_b.reshape(1, d).astype(F32), alpha, row_tile)


def kernel(x, w_in, kv_norm_g, w_uk, w_uv, log_dt, a_re, a_im, b_re, b_im, c_re, c_im, d_skip, w_glu, b_glu, w_out, ln_g, ln_b):
    depth = w_in.shape[0]
    alpha = (2 * depth) ** 0.25
    h = x
    for l in range(depth):
        h = _layer(h, alpha, (Q_TILE, RADIX_BITS, ROW_TILE, SCAN_TILE), w_in[l], kv_norm_g[l], w_uk[l], w_uv[l], log_dt[l], a_re[l], a_im[l],
                   b_re[l], b_im[l], c_re[l], c_im[l], d_skip[l], w_glu[l], b_glu[l],
                   w_out[l], ln_g[l], ln_b[l])
    return h
```

```python
import functools
import math

import jax
import jax.numpy as jnp
from jax import lax
from jax.experimental import pallas as pl
from jax.experimental.pallas import tpu as pltpu

ATT_HEADS = 8
ATT_HEAD_DIM = 64
ATT_WIDTH = ATT_HEADS * ATT_HEAD_DIM
KV_LATENT = 128
IDX_HEADS = 8
IDX_DIM = 32
TOPK_MAX = 256
SSM_WIDTH = 512
GROUP_CH = 16
N_GROUPS = SSM_WIDTH // GROUP_CH
STATE = 64
N_STATES = N_GROUPS * STATE
LN_EPS = 1e-5
RMS_EPS = 1e-6

LANES = 128
SUBLANES = 8
VMEM_LIMIT_BYTES = 56 * 1024 * 1024

Q_TILE = 256
RADIX_BITS = 1
ROW_TILE = 1024
SCAN_TILE = 64
KEY_TILE = 128
KV_TILE = 2 * KEY_TILE
INT_MIN = -(2 ** 31)
MASK_BIAS = -1e30
LOG2E = 1.4426950408889634

F32 = jnp.float32
BF16 = jnp.bfloat16
I32 = jnp.int32

_NT = (((1,), (1,)), ((), ()))


def _dot(a, b):
    return jnp.dot(a, b, preferred_element_type=F32)


def _dot_nt(a, b):
    return lax.dot_general(a, b, _NT, preferred_element_type=F32)


def _sigmoid(x):
    return 1.0 / (1.0 + jnp.exp(-x))


def _silu(x):
    return x * _sigmoid(x)


def _gelu_tanh(x):
    return 0.5 * x * (1.0 + jnp.tanh(math.sqrt(2.0 / math.pi) * (x + 0.044715 * (x * x * x))))


def _inproj_kernel(x_ref, wq_ref, wsm_ref, wqi_ref, wga_ref, wukt_ref, kvg_ref,
                   qlt_ref, qi_ref, ki_ref, wit_ref, c_ref, ct_ref, ga_ref, *, q_tile):
    x = x_ref[0].astype(BF16)
    q = _dot(x, wq_ref[...])
    q_scale = (ATT_HEAD_DIM ** -0.5) * LOG2E
    for h in range(ATT_HEADS):
        qh = q[:, h * ATT_HEAD_DIM:(h + 1) * ATT_HEAD_DIM].astype(BF16)
        ql_t = (_dot(qh, wukt_ref[h]) * q_scale).T.astype(BF16)
        for g in range(x.shape[0] // q_tile):
            qlt_ref[0, g, :, h * q_tile:(h + 1) * q_tile] = ql_t[:, g * q_tile:(g + 1) * q_tile]
    sm = _dot(x, wsm_ref[...])
    ckv = sm[:, :KV_LATENT]
    c = ckv * lax.rsqrt(jnp.mean(ckv * ckv, axis=-1, keepdims=True) + RMS_EPS) * kvg_ref[...]
    c_ref[0] = c.astype(BF16)
    c_t = c.T.astype(BF16)
    for g in range(x.shape[0] // KV_TILE):
        ct_ref[0, g] = c_t[:, g * KV_TILE:(g + 1) * KV_TILE]
    ki_ref[0] = sm[:, KV_LATENT:KV_LATENT + IDX_DIM].astype(BF16)
    sm_t = sm[:, KV_LATENT:].T
    wit_ref[0] = sm_t[IDX_DIM:IDX_DIM + IDX_HEADS, :] * (IDX_HEADS ** -0.5)
    qi = _dot(x, wqi_ref[...]) * (IDX_DIM ** -0.5)
    for h in range(IDX_HEADS):
        qi_ref[0, h] = qi[:, h * IDX_DIM:(h + 1) * IDX_DIM].astype(BF16)
    ga_ref[0] = _dot(x, wga_ref[...])


def _in_projection(x, wq, wsm, wqi, wga, wukt, kvg, tile, q_tile):
    bsz, seq, d = x.shape
    nt = seq // tile
    const2 = lambda b, i: (0, 0)
    const3 = lambda b, i: (0, 0, 0)
    out_shape = (
        jax.ShapeDtypeStruct((bsz, seq // q_tile, KV_LATENT, ATT_HEADS * q_tile), BF16),
        jax.ShapeDtypeStruct((bsz, IDX_HEADS, seq, IDX_DIM), BF16),
        jax.ShapeDtypeStruct((bsz, seq, IDX_DIM), BF16),
        jax.ShapeDtypeStruct((bsz, IDX_HEADS, seq), F32),
        jax.ShapeDtypeStruct((bsz, seq, KV_LATENT), BF16),
        jax.ShapeDtypeStruct((bsz, seq // KV_TILE, KV_LATENT, KV_TILE), BF16),
        jax.ShapeDtypeStruct((bsz, seq, ATT_WIDTH), F32),
    )
    out_specs = (
        pl.BlockSpec((1, tile // q_tile, KV_LATENT, ATT_HEADS * q_tile), lambda b, i: (b, i, 0, 0)),
        pl.BlockSpec((1, IDX_HEADS, tile, IDX_DIM), lambda b, i: (b, 0, i, 0)),
        pl.BlockSpec((1, tile, IDX_DIM), lambda b, i: (b, i, 0)),
        pl.BlockSpec((1, IDX_HEADS, tile), lambda b, i: (b, 0, i)),
        pl.BlockSpec((1, tile, KV_LATENT), lambda b, i: (b, i, 0)),
        pl.BlockSpec((1, tile // KV_TILE, KV_LATENT, KV_TILE), lambda b, i: (b, i, 0, 0)),
        pl.BlockSpec((1, tile, ATT_WIDTH), lambda b, i: (b, i, 0)),
    )
    in_specs = [
        pl.BlockSpec((1, tile, d), lambda b, i: (b, i, 0)),
        pl.BlockSpec(wq.shape, const2),
        pl.BlockSpec(wsm.shape, const2),
        pl.BlockSpec(wqi.shape, const2),
        pl.BlockSpec(wga.shape, const2),
        pl.BlockSpec(wukt.shape, const3),
        pl.BlockSpec(kvg.shape, const2),
    ]
    return pl.pallas_call(
        functools.partial(_inproj_kernel, q_tile=q_tile),
        grid=(bsz, nt),
        in_specs=in_specs,
        out_specs=out_specs,
        out_shape=out_shape,
        compiler_params=pltpu.CompilerParams(
            dimension_semantics=("arbitrary", "arbitrary"), vmem_limit_bytes=VMEM_LIMIT_BYTES),
        name="in_projection",
    )(x, wq, wsm, wqi, wga, wukt, kvg)


CNT_TILE = 512
PACK16 = 16
I16 = jnp.int16
I16_MIN = -(2 ** 15)
MAX_SEQ = 256 * PACK16


def _tree_sum(xs):
    xs = list(xs)
    while len(xs) > 1:
        nxt = [xs[i] + xs[i + 1] for i in range(0, len(xs) - 1, 2)]
        if len(xs) % 2:
            nxt.append(xs[-1])
        xs = nxt
    return xs[0]


def _radix_select16(src_scr, n_cnt_tiles, k_target, ge_all, q_tile, bits):
    groups = CNT_TILE // PACK16
    n_cand = 2 ** bits - 1
    one, zero = jnp.ones((), BF16), jnp.zeros((), BF16)

    def bit_pass(i, state):
        tau, c_ge, c_gt = state
        step = jnp.left_shift(jnp.int32(1), 16 - bits * (i + 1))
        cands = [tau + k * step for k in range(1, n_cand + 1)]
        cands16 = [jnp.broadcast_to(c, (PACK16, q_tile)).astype(I16)[None] for c in cands]

        def count_tile(t, accs):
            r0 = pl.multiple_of(t * CNT_TILE, CNT_TILE)
            x = src_scr[pl.ds(r0, CNT_TILE), :].reshape(groups, PACK16, q_tile)
            out = []
            for acc, c16 in zip(accs, cands16):
                ge = jnp.where(x >= c16, one, zero)
                out.append(acc + _tree_sum([ge[r] for r in range(groups)]))
            return tuple(out)

        accs = lax.fori_loop(0, n_cnt_tiles, count_tile,
                             tuple(jnp.zeros((PACK16, q_tile), BF16) for _ in range(n_cand)))
        counts = [jnp.sum(a.astype(F32), axis=0, keepdims=True).astype(I32) for a in accs]
        new_tau, new_ge, new_gt = tau, c_ge, counts[0]
        for k in range(n_cand):
            ok = counts[k] >= k_target
            above = counts[k + 1] if k + 1 < n_cand else c_gt
            new_tau = jnp.where(ok, cands[k], new_tau)
            new_ge = jnp.where(ok, counts[k], new_ge)
            new_gt = jnp.where(ok, above, new_gt)
        return new_tau, new_ge, new_gt

    init = (jnp.full((1, q_tile), I16_MIN, I32), ge_all, jnp.zeros((1, q_tile), I32))
    return lax.fori_loop(0, 16 // bits, bit_pass, init)


def _attention_kernel(qi_ref, wit_ref, ki_ref, qlt_ref, c_ref, ct_ref, wuvt_ref, out_ref,
                      key_scr, khi_scr, klo_scr, ksel_scr, bias_scr,
                      sa_scr, sb_scr, pa_scr, pb_scr, aa_scr, ab_scr, m_scr, acc_scr, *, n_sel, idx_bits, q_tile, bits):
    j = pl.program_id(1)
    n_key_tiles = (j + 1) * (q_tile // KEY_TILE)
    n_cnt_tiles = (n_key_tiles + 3) // 4
    n_pairs = n_cnt_tiles

    q_pos = j * q_tile + lax.broadcasted_iota(I32, (KEY_TILE, q_tile), 1)
    k_off = lax.broadcasted_iota(I32, (KEY_TILE, q_tile), 0)

    def store_keys(r0, key):
        key_scr[pl.ds(r0, KEY_TILE), :] = key
        khi_scr[pl.ds(r0, KEY_TILE), :] = jnp.right_shift(key, 16).astype(I16)
        klo_scr[pl.ds(r0, KEY_TILE), :] = ((key & 0xFFFF) + I16_MIN).astype(I16)

    def score_tiles(i, carry):
        for sub in range(CNT_TILE // KEY_TILE):
            t = i * (CNT_TILE // KEY_TILE) + sub
            r0 = pl.multiple_of(t * KEY_TILE, KEY_TILE)
            ki_t = ki_ref[0, pl.ds(r0, KEY_TILE), :]
            score = jnp.zeros((KEY_TILE, q_tile), F32)
            for h in range(IDX_HEADS):
                logits = _dot_nt(ki_t, qi_ref[0, h])
                score = score + wit_ref[0, h:h + 1, :] * jnp.maximum(logits, 0.0)
            bits32 = pltpu.bitcast(score, I32)
            key = jnp.where(bits32 < 0, bits32 ^ jnp.int32(0x7FFFFFFF), bits32)
            key = jnp.where(score == 0.0, 0, key)
            store_keys(r0, jnp.where(t * KEY_TILE + k_off <= q_pos, key, INT_MIN))
        return carry

    lax.fori_loop(0, n_cnt_tiles, score_tiles, 0)

    k_sel = jnp.full((1, q_tile), n_sel, I32)
    n_rows = jnp.full((1, q_tile), n_cnt_tiles * CNT_TILE, I32)
    tau_hi, ge_hi, gt_hi = _radix_select16(khi_scr, n_cnt_tiles, k_sel, n_rows, q_tile, bits)
    tau_hi16 = jnp.broadcast_to(tau_hi, (PACK16, q_tile)).astype(I16)

    def sel_tile(t, carry):
        r0 = pl.multiple_of(t * CNT_TILE, CNT_TILE)
        hi = khi_scr[pl.ds(r0, CNT_TILE), :].reshape(CNT_TILE // PACK16, PACK16, q_tile)
        lo = klo_scr[pl.ds(r0, CNT_TILE), :].reshape(CNT_TILE // PACK16, PACK16, q_tile)
        sel = jnp.where(hi == tau_hi16[None], lo, jnp.full((), I16_MIN, I16))
        ksel_scr[pl.ds(r0, CNT_TILE), :] = sel.reshape(CNT_TILE, q_tile)
        return carry

    lax.fori_loop(0, n_cnt_tiles, sel_tile, 0)
    tau_lo, ge_lo, gt_lo = _radix_select16(ksel_scr, n_cnt_tiles, k_sel - gt_hi, ge_hi - gt_hi, q_tile, bits)
    tau = jnp.left_shift(tau_hi, 16) | (tau_lo - I16_MIN)
    need = n_sel - (gt_hi + gt_lo)
    excess = ((ge_lo - gt_lo) > need) & (tau != INT_MIN)

    @pl.when(jnp.max(jnp.where(excess, 1, 0)) > 0)
    def _():
        def idx_pass(i, lo_idx):
            cand = lo_idx + jnp.left_shift(jnp.int32(1), idx_bits - 1 - i)

            def count_tile(t, acc):
                r0 = pl.multiple_of(t * KEY_TILE, KEY_TILE)
                eq = jnp.where(key_scr[pl.ds(r0, KEY_TILE), :] == tau,
                               jnp.where(t * KEY_TILE + k_off < cand, 1, 0), 0)
                return acc + jnp.sum(eq.reshape(KEY_TILE // SUBLANES, SUBLANES, q_tile), axis=0)

            acc = lax.fori_loop(0, n_key_tiles, count_tile, jnp.zeros((SUBLANES, q_tile), I32))
            return jnp.where(jnp.sum(acc, axis=0, keepdims=True) < need, cand, lo_idx)

        cut = lax.fori_loop(0, idx_bits, idx_pass, jnp.zeros((1, q_tile), I32))

        def drop_tile(t, carry):
            r0 = pl.multiple_of(t * KEY_TILE, KEY_TILE)
            key = key_scr[pl.ds(r0, KEY_TILE), :]
            late = jnp.where(key == tau, jnp.where(t * KEY_TILE + k_off > cut, 1, 0), 0)
            key_scr[pl.ds(r0, KEY_TILE), :] = jnp.where(late == 1, INT_MIN, key)
            return carry

        lax.fori_loop(0, n_key_tiles, drop_tile, 0)

    ones_rows = jnp.ones((PACK16, KV_TILE), BF16)

    def scores(t, s_scr):
        r0 = pl.multiple_of(t * KV_TILE, KV_TILE)
        s_scr[...] = _dot(c_ref[0, pl.ds(r0, KV_TILE), :], qlt_ref[0, 0]).astype(BF16)

    def softmax(t, s_scr, p_scr, a_scr):
        for half in range(KV_TILE // KEY_TILE):
            kt = 2 * t + half
            r0 = pl.multiple_of(kt * KEY_TILE, KEY_TILE)
            bias_scr[half * KEY_TILE:(half + 1) * KEY_TILE, :] = jnp.where(
                key_scr[pl.ds(r0, KEY_TILE), :] >= tau,
                jnp.where(kt * KEY_TILE + k_off <= q_pos, 0.0, MASK_BIAS), MASK_BIAS).astype(BF16)
        for h in range(ATT_HEADS):
            cols = slice(h * q_tile, (h + 1) * q_tile)
            s = s_scr[:, cols] + bias_scr[...]
            m_prev = m_scr[:, cols]
            m_new = jnp.maximum(m_prev, jnp.max(s, axis=0, keepdims=True).astype(F32))
            a_scr[:, cols] = jnp.exp2(m_prev - m_new)
            p_scr[:, cols] = jnp.exp2(s - m_new.astype(BF16))
            m_scr[:, cols] = m_new

    def values(t, p_scr, a_scr):
        c_aug_t = jnp.concatenate([ct_ref[0, t], ones_rows], axis=0)
        acc_scr[...] = a_scr[...] * acc_scr[...] + _dot(c_aug_t, p_scr[...])

    m_scr[...] = jnp.full(m_scr.shape, MASK_BIAS, F32)
    acc_scr[...] = jnp.zeros(acc_scr.shape, F32)

    def pair(u, carry):
        for t, s_scr, p_scr, a_scr in ((2 * u, sa_scr, pa_scr, aa_scr), (2 * u + 1, sb_scr, pb_scr, ab_scr)):
            scores(t, s_scr)
            softmax(t, s_scr, p_scr, a_scr)
            values(t, p_scr, a_scr)
        return carry

    lax.fori_loop(0, n_pairs, pair, 0)

    inv_l = 1.0 / acc_scr[KV_LATENT:KV_LATENT + 1, :]
    for h in range(ATT_HEADS):
        cols = slice(h * q_tile, (h + 1) * q_tile)
        o_lat_t = (acc_scr[:KV_LATENT, cols] * inv_l[:, cols]).astype(BF16)
        out_ref[0, :, h * ATT_HEAD_DIM:(h + 1) * ATT_HEAD_DIM] = _dot(wuvt_ref[h], o_lat_t).T


def _sparse_attention(qlt, qi, ki, wit, c, ct, wuvt, q_tile=Q_TILE, bits=1):
    bsz, seq, _ = c.shape
    assert seq % CNT_TILE == 0 and seq <= MAX_SEQ, seq
    nq = seq // q_tile
    n_sel = min(TOPK_MAX, seq // 4)
    cols = ATT_HEADS * q_tile
    kernel = functools.partial(_attention_kernel, n_sel=n_sel, idx_bits=max(1, (seq - 1).bit_length()),
                               q_tile=q_tile, bits=bits)
    return pl.pallas_call(
        kernel,
        grid=(bsz, nq),
        in_specs=[
            pl.BlockSpec((1, IDX_HEADS, q_tile, IDX_DIM), lambda b, j: (b, 0, j, 0)),
            pl.BlockSpec((1, IDX_HEADS, q_tile), lambda b, j: (b, 0, j)),
            pl.BlockSpec((1, seq, IDX_DIM), lambda b, j: (b, 0, 0)),
            pl.BlockSpec((1, 1, KV_LATENT, cols), lambda b, j: (b, j, 0, 0)),
            pl.BlockSpec((1, seq, KV_LATENT), lambda b, j: (b, 0, 0)),
            pl.BlockSpec((1, seq // KV_TILE, KV_LATENT, KV_TILE), lambda b, j: (b, 0, 0, 0)),
            pl.BlockSpec(wuvt.shape, lambda b, j: (0, 0, 0)),
        ],
        out_specs=pl.BlockSpec((1, q_tile, ATT_WIDTH), lambda b, j: (b, j, 0)),
        out_shape=jax.ShapeDtypeStruct((bsz, seq, ATT_WIDTH), F32),
        scratch_shapes=[
            pltpu.VMEM((seq, q_tile), I32),
            pltpu.VMEM((seq, q_tile), I16),
            pltpu.VMEM((seq, q_tile), I16),
            pltpu.VMEM((seq, q_tile), I16),
            pltpu.VMEM((KV_TILE, q_tile), BF16),
            pltpu.VMEM((KV_TILE, cols), BF16),
            pltpu.VMEM((KV_TILE, cols), BF16),
            pltpu.VMEM((KV_TILE, cols), BF16),
            pltpu.VMEM((KV_TILE, cols), BF16),
            pltpu.VMEM((1, cols), F32),
            pltpu.VMEM((1, cols), F32),
            pltpu.VMEM((1, cols), F32),
            pltpu.VMEM((KV_LATENT + PACK16, cols), F32),
        ],
        compiler_params=pltpu.CompilerParams(
            dimension_semantics=("arbitrary", "arbitrary"), vmem_limit_bytes=VMEM_LIMIT_BYTES),
        name="sparse_attention",
    )(qi, wit, ki, qlt, c, ct, wuvt)


HALF_STATES = N_STATES // 2
HALF_COLS = 2 * HALF_STATES
SCAN_COLS = 512
SCAN_UNROLL = 8


def _s5_kernel(xin_ref, wu_ref, wgs_ref, bmat_ref, a_ref, cmat_ref, dskip_ref, wglu_ref, bglu_ref, out_ref,
               x_scr, state_scr, u_scr, gs_scr, *, scan_cols, unroll):
    bsz, tile, d = xin_ref.shape
    rows = tile * bsz

    @pl.when(pl.program_id(0) == 0)
    def _():
        state_scr[...] = jnp.zeros(state_scr.shape, F32)

    xin = xin_ref[...].reshape(rows, d).astype(BF16)
    u_bt = _dot(xin, wu_ref[...])
    gs_bt = _dot(xin, wgs_ref[...])
    for b in range(bsz):
        u_scr[:, b, :] = u_bt[b * tile:(b + 1) * tile, :]
        gs_scr[:, b, :] = gs_bt[b * tile:(b + 1) * tile, :]
    u = u_scr[...].reshape(rows, SSM_WIDTH)
    ub = u.astype(BF16)
    half_in = SSM_WIDTH // 2
    x_scr[:, :HALF_COLS] = _dot(ub[:, :half_in], bmat_ref[0])
    x_scr[:, HALF_COLS:] = _dot(ub[:, half_in:], bmat_ref[1])

    for half in range(2):
        for off in range(0, HALF_STATES, scan_cols):
            n0 = half * HALF_STATES + off
            cr = half * HALF_COLS + off
            ci = cr + HALF_STATES
            a_re = jnp.broadcast_to(a_ref[0:1, n0:n0 + scan_cols], (bsz, scan_cols))
            a_im = jnp.broadcast_to(a_ref[1:2, n0:n0 + scan_cols], (bsz, scan_cols))

            def step(t, carry, cr=cr, ci=ci, a_re=a_re, a_im=a_im):
                x_re, x_im = carry
                r0 = pl.multiple_of(t * bsz, bsz)
                n_re = a_re * x_re - a_im * x_im + x_scr[pl.ds(r0, bsz), cr:cr + scan_cols]
                n_im = a_re * x_im + a_im * x_re + x_scr[pl.ds(r0, bsz), ci:ci + scan_cols]
                x_scr[pl.ds(r0, bsz), cr:cr + scan_cols] = n_re
                x_scr[pl.ds(r0, bsz), ci:ci + scan_cols] = n_im
                return n_re, n_im

            x_re, x_im = lax.fori_loop(
                0, tile, step,
                (state_scr[:, cr:cr + scan_cols], state_scr[:, ci:ci + scan_cols]), unroll=unroll)
            state_scr[:, cr:cr + scan_cols] = x_re
            state_scr[:, ci:ci + scan_cols] = x_im

    y0 = _dot(x_scr[:, :HALF_COLS].astype(BF16), cmat_ref[0])
    y1 = _dot(x_scr[:, HALF_COLS:].astype(BF16), cmat_ref[1])
    y = jnp.concatenate([y0, y1], axis=1) + dskip_ref[...] * u
    y = _gelu_tanh(y)
    hcat = _dot(y.astype(BF16), wglu_ref[...]) + bglu_ref[...]
    ssm = hcat[:, :SSM_WIDTH] * _sigmoid(hcat[:, SSM_WIDTH:])
    gs = gs_scr[...].reshape(rows, SSM_WIDTH)
    out_ref[...] = (ssm * _silu(gs)).astype(BF16).reshape(tile, bsz, SSM_WIDTH)


def _s5_branch(x, wu, wgs, bmat, a_bar, cmat, dskip, wglu, bglu, tile, scan_cols=SCAN_COLS, unroll=SCAN_UNROLL):
    bsz, seq, d = x.shape
    const2 = lambda i: (0, 0)
    const3 = lambda i: (0, 0, 0)
    return pl.pallas_call(
        functools.partial(_s5_kernel, scan_cols=scan_cols, unroll=unroll),
        grid=(seq // tile,),
        in_specs=[
            pl.BlockSpec((bsz, tile, d), lambda i: (0, i, 0)),
            pl.BlockSpec(wu.shape, const2),
            pl.BlockSpec(wgs.shape, const2),
            pl.BlockSpec(bmat.shape, const3),
            pl.BlockSpec(a_bar.shape, const2),
            pl.BlockSpec(cmat.shape, const3),
            pl.BlockSpec(dskip.shape, const2),
            pl.BlockSpec(wglu.shape, const2),
            pl.BlockSpec(bglu.shape, const2),
        ],
        out_specs=pl.BlockSpec((tile, bsz, SSM_WIDTH), lambda i: (i, 0, 0)),
        out_shape=jax.ShapeDtypeStruct((seq, bsz, SSM_WIDTH), BF16),
        scratch_shapes=[
            pltpu.VMEM((tile * bsz, 2 * N_STATES), F32),
            pltpu.VMEM((bsz, 2 * N_STATES), F32),
            pltpu.VMEM((tile, bsz, SSM_WIDTH), F32),
            pltpu.VMEM((tile, bsz, SSM_WIDTH), F32),
        ],
        compiler_params=pltpu.CompilerParams(
            dimension_semantics=("arbitrary",), vmem_limit_bytes=VMEM_LIMIT_BYTES),
        name="s5_branch",
    )(x, wu, wgs, bmat, a_bar, cmat, dskip, wglu, bglu)


def _s5_matrices(log_dt, a_re, a_im, b_re, b_im, c_re, c_im):
    dt = jnp.exp(log_dt.astype(F32))[:, None]
    lam_re, lam_im = a_re.astype(F32), a_im.astype(F32)
    mag = jnp.exp(lam_re * dt)
    ab_re = mag * jnp.cos(lam_im * dt)
    ab_im = mag * jnp.sin(lam_im * dt)
    den = lam_re * lam_re + lam_im * lam_im
    k_re = ((ab_re - 1.0) * lam_re + ab_im * lam_im) / den
    k_im = (ab_im * lam_re - (ab_re - 1.0) * lam_im) / den
    bb_re = k_re[..., None] * b_re - k_im[..., None] * b_im
    bb_im = k_re[..., None] * b_im + k_im[..., None] * b_re
    gh = N_GROUPS // 2
    eye = jnp.eye(gh, dtype=F32)

    def in_block(bb):
        return jnp.einsum('gpc,gh->gchp', bb, eye).reshape(gh * GROUP_CH, gh * STATE)

    def out_block(cc):
        return jnp.einsum('gcp,gh->gphc', cc, eye).reshape(gh * STATE, gh * GROUP_CH)

    bmat = jnp.stack([
        jnp.concatenate([in_block(bb_re[k * gh:(k + 1) * gh]), in_block(bb_im[k * gh:(k + 1) * gh])], axis=1)
        for k in range(2)])
    cmat = jnp.stack([
        jnp.concatenate([out_block(c_re[k * gh:(k + 1) * gh]), out_block(-c_im[k * gh:(k + 1) * gh])], axis=0)
        for k in range(2)])
    a_bar = jnp.stack([ab_re.reshape(N_STATES), ab_im.reshape(N_STATES)])
    return bmat.astype(BF16), a_bar, cmat.astype(BF16)


def _outproj_kernel(att_ref, ga_ref, ssm_ref, x_ref, wo_ref, lng_ref, lnb_ref, out_ref, *, alpha):
    att = att_ref[0] * _silu(ga_ref[0])
    y = _dot(att.astype(BF16), wo_ref[:ATT_WIDTH, :]) + _dot(ssm_ref[...], wo_ref[ATT_WIDTH:, :])
    z = alpha * x_ref[0] + y
    mu = jnp.mean(z, axis=-1, keepdims=True)
    zc = z - mu
    var = jnp.mean(zc * zc, axis=-1, keepdims=True)
    out_ref[0] = zc * lax.rsqrt(var + LN_EPS) * lng_ref[...] + lnb_ref[...]


def _out_projection(att, ga, ssm_t, x, wo, lng, lnb, alpha, tile):
    bsz, seq, d = x.shape
    const2 = lambda b, i: (0, 0)
    return pl.pallas_call(
        functools.partial(_outproj_kernel, alpha=alpha),
        grid=(bsz, seq // tile),
        in_specs=[
            pl.BlockSpec((1, tile, ATT_WIDTH), lambda b, i: (b, i, 0)),
            pl.BlockSpec((1, tile, ATT_WIDTH), lambda b, i: (b, i, 0)),
            pl.BlockSpec((tile, SSM_WIDTH), lambda b, i: (i, b)),
            pl.BlockSpec((1, tile, d), lambda b, i: (b, i, 0)),
            pl.BlockSpec(wo.shape, const2),
            pl.BlockSpec(lng.shape, const2),
            pl.BlockSpec(lnb.shape, const2),
        ],
        out_specs=pl.BlockSpec((1, tile, d), lambda b, i: (b, i, 0)),
        out_shape=jax.ShapeDtypeStruct((bsz, seq, d), F32),
        compiler_params=pltpu.CompilerParams(
            dimension_semantics=("arbitrary", "arbitrary"), vmem_limit_bytes=VMEM_LIMIT_BYTES),
        name="out_projection",
    )(att, ga, ssm_t, x, wo, lng, lnb)


def _split_w_in(w_in):
    sizes = (ATT_WIDTH, KV_LATENT, IDX_HEADS * IDX_DIM, IDX_DIM, IDX_HEADS, ATT_WIDTH, SSM_WIDTH, SSM_WIDTH)
    parts, acc = [], 0
    for s in sizes:
        parts.append(w_in[:, acc:acc + s])
        acc += s
    wq, wckv, wqi, wki, wwi, wga, wu, wgs = parts
    pad = jnp.zeros((w_in.shape[0], 2 * LANES - KV_LATENT - IDX_DIM - IDX_HEADS), w_in.dtype)
    wsm = jnp.concatenate([wckv, wki, wwi, pad], axis=1)
    return [w.astype(BF16) for w in (wq, wsm, wqi, wga, wu, wgs)]


def _layer(x, alpha, cfg, w_in, kv_g, w_uk, w_uv, log_dt, a_re, a_im, b_re, b_im, c_re, c_im, d_skip,
           w_glu, b_glu, w_out, ln_g, ln_b):
    bsz, seq, d = x.shape
    row_tile = min(cfg[2], seq)
    scan_tile = min(cfg[3], seq)
    wq, wsm, wqi, wga, wu, wgs = _split_w_in(w_in)
    wukt = jnp.swapaxes(w_uk, 1, 2).astype(BF16)
    qlt, qi, ki, wit, c, ct, ga = _in_projection(
        x, wq, wsm, wqi, wga, wukt, kv_g.reshape(1, KV_LATENT).astype(F32), row_tile, cfg[0])
    att = _sparse_attention(qlt, qi, ki, wit, c, ct, jnp.swapaxes(w_uv, 1, 2).astype(BF16), cfg[0], cfg[1])
    bmat, a_bar, cmat = _s5_matrices(log_dt, a_re, a_im, b_re, b_im, c_re, c_im)
    ssm_t = _s5_branch(
        x, wu, wgs, bmat, a_bar, cmat,
        d_skip.reshape(1, SSM_WIDTH).astype(F32), w_glu.astype(BF16),
        b_glu.reshape(1, 2 * SSM_WIDTH).astype(F32), scan_tile)
    return _out_projection(
        att, ga, ssm_t.reshape(seq, bsz * SSM_WIDTH), x, w_out.astype(BF16),
        ln_g.reshape(1, d).astype(F32), ln_b.reshape(1, d).astype(F32), alpha, row_tile)


def kernel(x, w_in, kv_norm_g, w_uk, w_uv, log_dt, a_re, a_im, b_re, b_im, c_re, c_im, d_skip, w_glu, b_glu, w_out, ln_g, ln_b):
    depth = w_in.shape[0]
    alpha = (2 * depth) ** 0.25
    h = x
    for l in range(depth):
        h = _layer(h, alpha, (Q_TILE, RADIX_BITS, ROW_TILE, SCAN_TILE), w_in[l], kv_norm_g[l], w_uk[l], w_uv[l], log_dt[l], a_re[l], a_im[l],
                   b_re[l], b_im[l], c_re[l], c_im[l], d_skip[l], w_glu[l], b_glu[l],
                   w_out[l], ln_g[l], ln_b[l])
    return h
```

```python
import functools
import math

import jax
import jax.numpy as jnp
from jax import lax
from jax.experimental import pallas as pl
from jax.experimental.pallas import tpu as pltpu

ATT_HEADS = 8
ATT_HEAD_DIM = 64
ATT_WIDTH = ATT_HEADS * ATT_HEAD_DIM
KV_LATENT = 128
IDX_HEADS = 8
IDX_DIM = 32
TOPK_MAX = 256
SSM_WIDTH = 512
GROUP_CH = 16
N_GROUPS = SSM_WIDTH // GROUP_CH
STATE = 64
N_STATES = N_GROUPS * STATE
LN_EPS = 1e-5
RMS_EPS = 1e-6

LANES = 128
SUBLANES = 8
VMEM_LIMIT_BYTES = 56 * 1024 * 1024

Q_TILE = 256
RADIX_BITS = 1
ROW_TILE = 1024
SCAN_TILE = 64
KEY_TILE = 128
KV_TILE = 4 * KEY_TILE
INT_MIN = -(2 ** 31)
MASK_BIAS = -1e30
LOG2E = 1.4426950408889634

F32 = jnp.float32
BF16 = jnp.bfloat16
I32 = jnp.int32

_NT = (((1,), (1,)), ((), ()))


def _dot(a, b):
    return jnp.dot(a, b, preferred_element_type=F32)


def _dot_nt(a, b):
    return lax.dot_general(a, b, _NT, preferred_element_type=F32)


def _sigmoid(x):
    return 1.0 / (1.0 + jnp.exp(-x))


def _silu(x):
    return x * _sigmoid(x)


def _gelu_tanh(x):
    return 0.5 * x * (1.0 + jnp.tanh(math.sqrt(2.0 / math.pi) * (x + 0.044715 * (x * x * x))))


def _inproj_kernel(x_ref, wq_ref, wsm_ref, wqi_ref, wga_ref, wukt_ref, kvg_ref,
                   qlt_ref, qi_ref, ki_ref, wit_ref, c_ref, ct_ref, ga_ref, *, q_tile):
    x = x_ref[0].astype(BF16)
    q = _dot(x, wq_ref[...])
    q_scale = (ATT_HEAD_DIM ** -0.5) * LOG2E
    for h in range(ATT_HEADS):
        qh = q[:, h * ATT_HEAD_DIM:(h + 1) * ATT_HEAD_DIM].astype(BF16)
        ql_t = (_dot(qh, wukt_ref[h]) * q_scale).T.astype(BF16)
        for g in range(x.shape[0] // q_tile):
            qlt_ref[0, g, :, h * q_tile:(h + 1) * q_tile] = ql_t[:, g * q_tile:(g + 1) * q_tile]
    sm = _dot(x, wsm_ref[...])
    ckv = sm[:, :KV_LATENT]
    c = ckv * lax.rsqrt(jnp.mean(ckv * ckv, axis=-1, keepdims=True) + RMS_EPS) * kvg_ref[...]
    c_ref[0] = c.astype(BF16)
    c_t = c.T.astype(BF16)
    for g in range(x.shape[0] // KV_TILE):
        ct_ref[0, g] = c_t[:, g * KV_TILE:(g + 1) * KV_TILE]
    ki_ref[0] = sm[:, KV_LATENT:KV_LATENT + IDX_DIM].astype(BF16)
    sm_t = sm[:, KV_LATENT:].T
    wit_ref[0] = sm_t[IDX_DIM:IDX_DIM + IDX_HEADS, :] * (IDX_HEADS ** -0.5)
    qi = _dot(x, wqi_ref[...]) * (IDX_DIM ** -0.5)
    for h in range(IDX_HEADS):
        qi_ref[0, h] = qi[:, h * IDX_DIM:(h + 1) * IDX_DIM].astype(BF16)
    ga_ref[0] = _dot(x, wga_ref[...])


def _in_projection(x, wq, wsm, wqi, wga, wukt, kvg, tile, q_tile):
    bsz, seq, d = x.shape
    nt = seq // tile
    const2 = lambda b, i: (0, 0)
    const3 = lambda b, i: (0, 0, 0)
    out_shape = (
        jax.ShapeDtypeStruct((bsz, seq // q_tile, KV_LATENT, ATT_HEADS * q_tile), BF16),
        jax.ShapeDtypeStruct((bsz, IDX_HEADS, seq, IDX_DIM), BF16),
        jax.ShapeDtypeStruct((bsz, seq, IDX_DIM), BF16),
        jax.ShapeDtypeStruct((bsz, IDX_HEADS, seq), F32),
        jax.ShapeDtypeStruct((bsz, seq, KV_LATENT), BF16),
        jax.ShapeDtypeStruct((bsz, seq // KV_TILE, KV_LATENT, KV_TILE), BF16),
        jax.ShapeDtypeStruct((bsz, seq, ATT_WIDTH), F32),
    )
    out_specs = (
        pl.BlockSpec((1, tile // q_tile, KV_LATENT, ATT_HEADS * q_tile), lambda b, i: (b, i, 0, 0)),
        pl.BlockSpec((1, IDX_HEADS, tile, IDX_DIM), lambda b, i: (b, 0, i, 0)),
        pl.BlockSpec((1, tile, IDX_DIM), lambda b, i: (b, i, 0)),
        pl.BlockSpec((1, IDX_HEADS, tile), lambda b, i: (b, 0, i)),
        pl.BlockSpec((1, tile, KV_LATENT), lambda b, i: (b, i, 0)),
        pl.BlockSpec((1, tile // KV_TILE, KV_LATENT, KV_TILE), lambda b, i: (b, i, 0, 0)),
        pl.BlockSpec((1, tile, ATT_WIDTH), lambda b, i: (b, i, 0)),
    )
    in_specs = [
        pl.BlockSpec((1, tile, d), lambda b, i: (b, i, 0)),
        pl.BlockSpec(wq.shape, const2),
        pl.BlockSpec(wsm.shape, const2),
        pl.BlockSpec(wqi.shape, const2),
        pl.BlockSpec(wga.shape, const2),
        pl.BlockSpec(wukt.shape, const3),
        pl.BlockSpec(kvg.shape, const2),
    ]
    return pl.pallas_call(
        functools.partial(_inproj_kernel, q_tile=q_tile),
        grid=(bsz, nt),
        in_specs=in_specs,
        out_specs=out_specs,
        out_shape=out_shape,
        compiler_params=pltpu.CompilerParams(
            dimension_semantics=("arbitrary", "arbitrary"), vmem_limit_bytes=VMEM_LIMIT_BYTES),
        name="in_projection",
    )(x, wq, wsm, wqi, wga, wukt, kvg)


CNT_TILE = 512
PACK16 = 16
I16 = jnp.int16
I16_MIN = -(2 ** 15)
MAX_SEQ = 256 * PACK16


def _tree_sum(xs):
    xs = list(xs)
    while len(xs) > 1:
        nxt = [xs[i] + xs[i + 1] for i in range(0, len(xs) - 1, 2)]
        if len(xs) % 2:
            nxt.append(xs[-1])
        xs = nxt
    return xs[0]


def _radix_select16(src_scr, n_cnt_tiles, k_target, ge_all, q_tile, bits):
    groups = CNT_TILE // PACK16
    n_cand = 2 ** bits - 1
    one, zero = jnp.ones((), BF16), jnp.zeros((), BF16)

    def bit_pass(i, state):
        tau, c_ge, c_gt = state
        step = jnp.left_shift(jnp.int32(1), 16 - bits * (i + 1))
        cands = [tau + k * step for k in range(1, n_cand + 1)]
        cands16 = [jnp.broadcast_to(c, (PACK16, q_tile)).astype(I16)[None] for c in cands]

        def count_tile(t, accs):
            r0 = pl.multiple_of(t * CNT_TILE, CNT_TILE)
            x = src_scr[pl.ds(r0, CNT_TILE), :].reshape(groups, PACK16, q_tile)
            out = []
            for acc, c16 in zip(accs, cands16):
                ge = jnp.where(x >= c16, one, zero)
                out.append(acc + _tree_sum([ge[r] for r in range(groups)]))
            return tuple(out)

        accs = lax.fori_loop(0, n_cnt_tiles, count_tile,
                             tuple(jnp.zeros((PACK16, q_tile), BF16) for _ in range(n_cand)))
        counts = [jnp.sum(a.astype(F32), axis=0, keepdims=True).astype(I32) for a in accs]
        new_tau, new_ge, new_gt = tau, c_ge, counts[0]
        for k in range(n_cand):
            ok = counts[k] >= k_target
            above = counts[k + 1] if k + 1 < n_cand else c_gt
            new_tau = jnp.where(ok, cands[k], new_tau)
            new_ge = jnp.where(ok, counts[k], new_ge)
            new_gt = jnp.where(ok, above, new_gt)
        return new_tau, new_ge, new_gt

    init = (jnp.full((1, q_tile), I16_MIN, I32), ge_all, jnp.zeros((1, q_tile), I32))
    return lax.fori_loop(0, 16 // bits, bit_pass, init)


def _attention_kernel(qi_ref, wit_ref, ki_ref, qlt_ref, c_ref, ct_ref, wuvt_ref, out_ref,
                      key_scr, khi_scr, klo_scr, ksel_scr, bias_scr,
                      s_scr, p_scr, a_scr, m_scr, acc_scr, *, n_sel, idx_bits, q_tile, bits):
    j = pl.program_id(1)
    n_key_tiles = (j + 1) * (q_tile // KEY_TILE)
    n_cnt_tiles = (n_key_tiles + 3) // 4

    q_pos = j * q_tile + lax.broadcasted_iota(I32, (KEY_TILE, q_tile), 1)
    k_off = lax.broadcasted_iota(I32, (KEY_TILE, q_tile), 0)

    def store_keys(r0, key):
        key_scr[pl.ds(r0, KEY_TILE), :] = key
        khi_scr[pl.ds(r0, KEY_TILE), :] = jnp.right_shift(key, 16).astype(I16)
        klo_scr[pl.ds(r0, KEY_TILE), :] = ((key & 0xFFFF) + I16_MIN).astype(I16)

    def score_tiles(i, carry):
        for sub in range(CNT_TILE // KEY_TILE):
            t = i * (CNT_TILE // KEY_TILE) + sub
            r0 = pl.multiple_of(t * KEY_TILE, KEY_TILE)
            ki_t = ki_ref[0, pl.ds(r0, KEY_TILE), :]
            score = jnp.zeros((KEY_TILE, q_tile), F32)
            for h in range(IDX_HEADS):
                logits = _dot_nt(ki_t, qi_ref[0, h])
                score = score + wit_ref[0, h:h + 1, :] * jnp.maximum(logits, 0.0)
            bits32 = pltpu.bitcast(score, I32)
            key = jnp.where(bits32 < 0, bits32 ^ jnp.int32(0x7FFFFFFF), bits32)
            key = jnp.where(score == 0.0, 0, key)
            store_keys(r0, jnp.where(t * KEY_TILE + k_off <= q_pos, key, INT_MIN))
        return carry

    lax.fori_loop(0, n_cnt_tiles, score_tiles, 0)

    k_sel = jnp.full((1, q_tile), n_sel, I32)
    n_rows = jnp.full((1, q_tile), n_cnt_tiles * CNT_TILE, I32)
    tau_hi, ge_hi, gt_hi = _radix_select16(khi_scr, n_cnt_tiles, k_sel, n_rows, q_tile, bits)
    tau_hi16 = jnp.broadcast_to(tau_hi, (PACK16, q_tile)).astype(I16)

    def sel_tile(t, carry):
        r0 = pl.multiple_of(t * CNT_TILE, CNT_TILE)
        hi = khi_scr[pl.ds(r0, CNT_TILE), :].reshape(CNT_TILE // PACK16, PACK16, q_tile)
        lo = klo_scr[pl.ds(r0, CNT_TILE), :].reshape(CNT_TILE // PACK16, PACK16, q_tile)
        sel = jnp.where(hi == tau_hi16[None], lo, jnp.full((), I16_MIN, I16))
        ksel_scr[pl.ds(r0, CNT_TILE), :] = sel.reshape(CNT_TILE, q_tile)
        return carry

    lax.fori_loop(0, n_cnt_tiles, sel_tile, 0)
    tau_lo, ge_lo, gt_lo = _radix_select16(ksel_scr, n_cnt_tiles, k_sel - gt_hi, ge_hi - gt_hi, q_tile, bits)
    tau = jnp.left_shift(tau_hi, 16) | (tau_lo - I16_MIN)
    need = n_sel - (gt_hi + gt_lo)
    excess = ((ge_lo - gt_lo) > need) & (tau != INT_MIN)

    @pl.when(jnp.max(jnp.where(excess, 1, 0)) > 0)
    def _():
        def idx_pass(i, lo_idx):
            cand = lo_idx + jnp.left_shift(jnp.int32(1), idx_bits - 1 - i)

            def count_tile(t, acc):
                r0 = pl.multiple_of(t * KEY_TILE, KEY_TILE)
                eq = jnp.where(key_scr[pl.ds(r0, KEY_TILE), :] == tau,
                               jnp.where(t * KEY_TILE + k_off < cand, 1, 0), 0)
                return acc + jnp.sum(eq.reshape(KEY_TILE // SUBLANES, SUBLANES, q_tile), axis=0)

            acc = lax.fori_loop(0, n_key_tiles, count_tile, jnp.zeros((SUBLANES, q_tile), I32))
            return jnp.where(jnp.sum(acc, axis=0, keepdims=True) < need, cand, lo_idx)

        cut = lax.fori_loop(0, idx_bits, idx_pass, jnp.zeros((1, q_tile), I32))

        def drop_tile(t, carry):
            r0 = pl.multiple_of(t * KEY_TILE, KEY_TILE)
            key = key_scr[pl.ds(r0, KEY_TILE), :]
            late = jnp.where(key == tau, jnp.where(t * KEY_TILE + k_off > cut, 1, 0), 0)
            key_scr[pl.ds(r0, KEY_TILE), :] = jnp.where(late == 1, INT_MIN, key)
            return carry

        lax.fori_loop(0, n_key_tiles, drop_tile, 0)

    ones_rows = jnp.ones((PACK16, KV_TILE), BF16)

    def scores(t, s_scr):
        r0 = pl.multiple_of(t * KV_TILE, KV_TILE)
        s_scr[...] = _dot(c_ref[0, pl.ds(r0, KV_TILE), :], qlt_ref[0, 0]).astype(BF16)

    def softmax(t, s_scr, p_scr, a_scr):
        for half in range(KV_TILE // KEY_TILE):
            kt = (KV_TILE // KEY_TILE) * t + half
            r0 = pl.multiple_of(kt * KEY_TILE, KEY_TILE)
            bias_scr[half * KEY_TILE:(half + 1) * KEY_TILE, :] = jnp.where(
                key_scr[pl.ds(r0, KEY_TILE), :] >= tau,
                jnp.where(kt * KEY_TILE + k_off <= q_pos, 0.0, MASK_BIAS), MASK_BIAS).astype(BF16)
        for h in range(ATT_HEADS):
            cols = slice(h * q_tile, (h + 1) * q_tile)
            s = s_scr[:, cols] + bias_scr[...]
            m_prev = m_scr[:, cols]
            m_new = jnp.maximum(m_prev, jnp.max(s, axis=0, keepdims=True).astype(F32))
            a_scr[:, cols] = jnp.exp2(m_prev - m_new)
            p_scr[:, cols] = jnp.exp2(s - m_new.astype(BF16))
            m_scr[:, cols] = m_new

    def values(t, p_scr, a_scr):
        c_aug_t = jnp.concatenate([ct_ref[0, t], ones_rows], axis=0)
        acc_scr[...] = a_scr[...] * acc_scr[...] + _dot(c_aug_t, p_scr[...])

    m_scr[...] = jnp.full(m_scr.shape, MASK_BIAS, F32)
    acc_scr[...] = jnp.zeros(acc_scr.shape, F32)

    def kv_tile(t, carry):
        scores(t, s_scr)
        softmax(t, s_scr, p_scr, a_scr)
        values(t, p_scr, a_scr)
        return carry

    lax.fori_loop(0, n_cnt_tiles, kv_tile, 0)

    inv_l = 1.0 / acc_scr[KV_LATENT:KV_LATENT + 1, :]
    for h in range(ATT_HEADS):
        cols = slice(h * q_tile, (h + 1) * q_tile)
        o_lat_t = (acc_scr[:KV_LATENT, cols] * inv_l[:, cols]).astype(BF16)
        out_ref[0, :, h * ATT_HEAD_DIM:(h + 1) * ATT_HEAD_DIM] = _dot(wuvt_ref[h], o_lat_t).T


def _sparse_attention(qlt, qi, ki, wit, c, ct, wuvt, q_tile=Q_TILE, bits=1):
    bsz, seq, _ = c.shape
    assert seq % CNT_TILE == 0 and seq <= MAX_SEQ, seq
    assert KV_TILE == CNT_TILE
    nq = seq // q_tile
    n_sel = min(TOPK_MAX, seq // 4)
    cols = ATT_HEADS * q_tile
    kernel = functools.partial(_attention_kernel, n_sel=n_sel, idx_bits=max(1, (seq - 1).bit_length()),
                               q_tile=q_tile, bits=bits)
    return pl.pallas_call(
        kernel,
        grid=(bsz, nq),
        in_specs=[
            pl.BlockSpec((1, IDX_HEADS, q_tile, IDX_DIM), lambda b, j: (b, 0, j, 0)),
            pl.BlockSpec((1, IDX_HEADS, q_tile), lambda b, j: (b, 0, j)),
            pl.BlockSpec((1, seq, IDX_DIM), lambda b, j: (b, 0, 0)),
            pl.BlockSpec((1, 1, KV_LATENT, cols), lambda b, j: (b, j, 0, 0)),
            pl.BlockSpec((1, seq, KV_LATENT), lambda b, j: (b, 0, 0)),
            pl.BlockSpec((1, seq // KV_TILE, KV_LATENT, KV_TILE), lambda b, j: (b, 0, 0, 0)),
            pl.BlockSpec(wuvt.shape, lambda b, j: (0, 0, 0)),
        ],
        out_specs=pl.BlockSpec((1, q_tile, ATT_WIDTH), lambda b, j: (b, j, 0)),
        out_shape=jax.ShapeDtypeStruct((bsz, seq, ATT_WIDTH), F32),
        scratch_shapes=[
            pltpu.VMEM((seq, q_tile), I32),
            pltpu.VMEM((seq, q_tile), I16),
            pltpu.VMEM((seq, q_tile), I16),
            pltpu.VMEM((seq, q_tile), I16),
            pltpu.VMEM((KV_TILE, q_tile), BF16),
            pltpu.VMEM((KV_TILE, cols), BF16),
            pltpu.VMEM((KV_TILE, cols), BF16),
            pltpu.VMEM((1, cols), F32),
            pltpu.VMEM((1, cols), F32),
            pltpu.VMEM((KV_LATENT + PACK16, cols), F32),
        ],
        compiler_params=pltpu.CompilerParams(
            dimension_semantics=("arbitrary", "arbitrary"), vmem_limit_bytes=VMEM_LIMIT_BYTES),
        name="sparse_attention",
    )(qi, wit, ki, qlt, c, ct, wuvt)


HALF_STATES = N_STATES // 2
HALF_COLS = 2 * HALF_STATES
SCAN_COLS = 512
SCAN_UNROLL = 8


def _s5_kernel(xin_ref, wu_ref, wgs_ref, bmat_ref, a_ref, cmat_ref, dskip_ref, wglu_ref, bglu_ref, out_ref,
               x_scr, state_scr, u_scr, gs_scr, *, scan_cols, unroll):
    bsz, tile, d = xin_ref.shape
    rows = tile * bsz

    @pl.when(pl.program_id(0) == 0)
    def _():
        state_scr[...] = jnp.zeros(state_scr.shape, F32)

    xin = xin_ref[...].reshape(rows, d).astype(BF16)
    u_bt = _dot(xin, wu_ref[...])
    gs_bt = _dot(xin, wgs_ref[...])
    for b in range(bsz):
        u_scr[:, b, :] = u_bt[b * tile:(b + 1) * tile, :]
        gs_scr[:, b, :] = gs_bt[b * tile:(b + 1) * tile, :]
    u = u_scr[...].reshape(rows, SSM_WIDTH)
    ub = u.astype(BF16)
    half_in = SSM_WIDTH // 2
    x_scr[:, :HALF_COLS] = _dot(ub[:, :half_in], bmat_ref[0])
    x_scr[:, HALF_COLS:] = _dot(ub[:, half_in:], bmat_ref[1])

    for half in range(2):
        for off in range(0, HALF_STATES, scan_cols):
            n0 = half * HALF_STATES + off
            cr = half * HALF_COLS + off
            ci = cr + HALF_STATES
            a_re = jnp.broadcast_to(a_ref[0:1, n0:n0 + scan_cols], (bsz, scan_cols))
            a_im = jnp.broadcast_to(a_ref[1:2, n0:n0 + scan_cols], (bsz, scan_cols))

            def step(t, carry, cr=cr, ci=ci, a_re=a_re, a_im=a_im):
                x_re, x_im = carry
                r0 = pl.multiple_of(t * bsz, bsz)
                n_re = a_re * x_re - a_im * x_im + x_scr[pl.ds(r0, bsz), cr:cr + scan_cols]
                n_im = a_re * x_im + a_im * x_re + x_scr[pl.ds(r0, bsz), ci:ci + scan_cols]
                x_scr[pl.ds(r0, bsz), cr:cr + scan_cols] = n_re
                x_scr[pl.ds(r0, bsz), ci:ci + scan_cols] = n_im
                return n_re, n_im

            x_re, x_im = lax.fori_loop(
                0, tile, step,
                (state_scr[:, cr:cr + scan_cols], state_scr[:, ci:ci + scan_cols]), unroll=unroll)
            state_scr[:, cr:cr + scan_cols] = x_re
            state_scr[:, ci:ci + scan_cols] = x_im

    y0 = _dot(x_scr[:, :HALF_COLS].astype(BF16), cmat_ref[0])
    y1 = _dot(x_scr[:, HALF_COLS:].astype(BF16), cmat_ref[1])
    y = jnp.concatenate([y0, y1], axis=1) + dskip_ref[...] * u
    y = _gelu_tanh(y)
    hcat = _dot(y.astype(BF16), wglu_ref[...]) + bglu_ref[...]
    ssm = hcat[:, :SSM_WIDTH] * _sigmoid(hcat[:, SSM_WIDTH:])
    gs = gs_scr[...].reshape(rows, SSM_WIDTH)
    out_ref[...] = (ssm * _silu(gs)).astype(BF16).reshape(tile, bsz, SSM_WIDTH)


def _s5_branch(x, wu, wgs, bmat, a_bar, cmat, dskip, wglu, bglu, tile, scan_cols=SCAN_COLS, unroll=SCAN_UNROLL):
    bsz, seq, d = x.shape
    const2 = lambda i: (0, 0)
    const3 = lambda i: (0, 0, 0)
    return pl.pallas_call(
        functools.partial(_s5_kernel, scan_cols=scan_cols, unroll=unroll),
        grid=(seq // tile,),
        in_specs=[
            pl.BlockSpec((bsz, tile, d), lambda i: (0, i, 0)),
            pl.BlockSpec(wu.shape, const2),
            pl.BlockSpec(wgs.shape, const2),
            pl.BlockSpec(bmat.shape, const3),
            pl.BlockSpec(a_bar.shape, const2),
            pl.BlockSpec(cmat.shape, const3),
            pl.BlockSpec(dskip.shape, const2),
            pl.BlockSpec(wglu.shape, const2),
            pl.BlockSpec(bglu.shape, const2),
        ],
        out_specs=pl.BlockSpec((tile, bsz, SSM_WIDTH), lambda i: (i, 0, 0)),
        out_shape=jax.ShapeDtypeStruct((seq, bsz, SSM_WIDTH), BF16),
        scratch_shapes=[
            pltpu.VMEM((tile * bsz, 2 * N_STATES), F32),
            pltpu.VMEM((bsz, 2 * N_STATES), F32),
            pltpu.VMEM((tile, bsz, SSM_WIDTH), F32),
            pltpu.VMEM((tile, bsz, SSM_WIDTH), F32),
        ],
        compiler_params=pltpu.CompilerParams(
            dimension_semantics=("arbitrary",), vmem_limit_bytes=VMEM_LIMIT_BYTES),
        name="s5_branch",
    )(x, wu, wgs, bmat, a_bar, cmat, dskip, wglu, bglu)


def _s5_matrices(log_dt, a_re, a_im, b_re, b_im, c_re, c_im):
    dt = jnp.exp(log_dt.astype(F32))[:, None]
    lam_re, lam_im = a_re.astype(F32), a_im.astype(F32)
    mag = jnp.exp(lam_re * dt)
    ab_re = mag * jnp.cos(lam_im * dt)
    ab_im = mag * jnp.sin(lam_im * dt)
    den = lam_re * lam_re + lam_im * lam_im
    k_re = ((ab_re - 1.0) * lam_re + ab_im * lam_im) / den
    k_im = (ab_im * lam_re - (ab_re - 1.0) * lam_im) / den
    bb_re = k_re[..., None] * b_re - k_im[..., None] * b_im
    bb_im = k_re[..., None] * b_im + k_im[..., None] * b_re
    gh = N_GROUPS // 2
    eye = jnp.eye(gh, dtype=F32)

    def in_block(bb):
        return jnp.einsum('gpc,gh->gchp', bb, eye).reshape(gh * GROUP_CH, gh * STATE)

    def out_block(cc):
        return jnp.einsum('gcp,gh->gphc', cc, eye).reshape(gh * STATE, gh * GROUP_CH)

    bmat = jnp.stack([
        jnp.concatenate([in_block(bb_re[k * gh:(k + 1) * gh]), in_block(bb_im[k * gh:(k + 1) * gh])], axis=1)
        for k in range(2)])
    cmat = jnp.stack([
        jnp.concatenate([out_block(c_re[k * gh:(k + 1) * gh]), out_block(-c_im[k * gh:(k + 1) * gh])], axis=0)
        for k in range(2)])
    a_bar = jnp.stack([ab_re.reshape(N_STATES), ab_im.reshape(N_STATES)])
    return bmat.astype(BF16), a_bar, cmat.astype(BF16)


def _outproj_kernel(att_ref, ga_ref, ssm_ref, x_ref, wo_ref, lng_ref, lnb_ref, out_ref, *, alpha):
    att = att_ref[0] * _silu(ga_ref[0])
    y = _dot(att.astype(BF16), wo_ref[:ATT_WIDTH, :]) + _dot(ssm_ref[...], wo_ref[ATT_WIDTH:, :])
    z = alpha * x_ref[0] + y
    mu = jnp.mean(z, axis=-1, keepdims=True)
    zc = z - mu
    var = jnp.mean(zc * zc, axis=-1, keepdims=True)
    out_ref[0] = zc * lax.rsqrt(var + LN_EPS) * lng_ref[...] + lnb_ref[...]


def _out_projection(att, ga, ssm_t, x, wo, lng, lnb, alpha, tile):
    bsz, seq, d = x.shape
    const2 = lambda b, i: (0, 0)
    return pl.pallas_call(
        functools.partial(_outproj_kernel, alpha=alpha),
        grid=(bsz, seq // tile),
        in_specs=[
            pl.BlockSpec((1, tile, ATT_WIDTH), lambda b, i: (b, i, 0)),
            pl.BlockSpec((1, tile, ATT_WIDTH), lambda b, i: (b, i, 0)),
            pl.BlockSpec((tile, SSM_WIDTH), lambda b, i: (i, b)),
            pl.BlockSpec((1, tile, d), lambda b, i: (b, i, 0)),
            pl.BlockSpec(wo.shape, const2),
            pl.BlockSpec(lng.shape, const2),
            pl.BlockSpec(lnb.shape, const2),
        ],
        out_specs=pl.BlockSpec((1, tile, d), lambda b, i: (b, i, 0)),
        out_shape=jax.ShapeDtypeStruct((bsz, seq, d), F32),
        compiler_params=pltpu.CompilerParams(
            dimension_semantics=("arbitrary", "arbitrary"), vmem_limit_bytes=VMEM_LIMIT_BYTES),
        name="out_projection",
    )(att, ga, ssm_t, x, wo, lng, lnb)


def _split_w_in(w_in):
    sizes = (ATT_WIDTH, KV_LATENT, IDX_HEADS * IDX_DIM, IDX_DIM, IDX_HEADS, ATT_WIDTH, SSM_WIDTH, SSM_WIDTH)
    parts, acc = [], 0
    for s in sizes:
        parts.append(w_in[:, acc:acc + s])
        acc += s
    wq, wckv, wqi, wki, wwi, wga, wu, wgs = parts
    pad = jnp.zeros((w_in.shape[0], 2 * LANES - KV_LATENT - IDX_DIM - IDX_HEADS), w_in.dtype)
    wsm = jnp.concatenate([wckv, wki, wwi, pad], axis=1)
    return [w.astype(BF16) for w in (wq, wsm, wqi, wga, wu, wgs)]


def _layer(x, alpha, cfg, w_in, kv_g, w_uk, w_uv, log_dt, a_re, a_im, b_re, b_im, c_re, c_im, d_skip,
           w_glu, b_glu, w_out, ln_g, ln_b):
    bsz, seq, d = x.shape
    row_tile = min(cfg[2], seq)
    scan_tile = min(cfg[3], seq)
    wq, wsm, wqi, wga, wu, wgs = _split_w_in(w_in)
    wukt = jnp.swapaxes(w_uk, 1, 2).astype(BF16)
    qlt, qi, ki, wit, c, ct, ga = _in_projection(
        x, wq, wsm, wqi, wga, wukt, kv_g.reshape(1, KV_LATENT).astype(F32), row_tile, cfg[0])
    att = _sparse_attention(qlt, qi, ki, wit, c, ct, jnp.swapaxes(w_uv, 1, 2).astype(BF16), cfg[0], cfg[1])
    bmat, a_bar, cmat = _s5_matrices(log_dt, a_re, a_im, b_re, b_im, c_re, c_im)
    ssm_t = _s5_branch(
        x, wu, wgs, bmat, a_bar, cmat,
        d_skip.reshape(1, SSM_WIDTH).astype(F32), w_glu.astype(BF16),
        b_glu.reshape(1, 2 * SSM_WIDTH).astype(F32), scan_tile)
    return _out_projection(
        att, ga, ssm_t.reshape(seq, bsz * SSM_WIDTH), x, w_out.astype(BF16),
        ln_g.reshape(1, d).astype(F32), ln_b.reshape(1, d).astype(F32), alpha, row_tile)


def kernel(x, w_in, kv_norm_g, w_uk, w_uv, log_dt, a_re, a_im, b_re, b_im, c_re, c_im, d_skip, w_glu, b_glu, w_out, ln_g, ln_b):
    depth = w_in.shape[0]
    alpha = (2 * depth) ** 0.25
    h = x
    for l in range(depth):
        h = _layer(h, alpha, (Q_TILE, RADIX_BITS, ROW_TILE, SCAN_TILE), w_in[l], kv_norm_g[l], w_uk[l], w_uv[l], log_dt[l], a_re[l], a_im[l],
                   b_re[l], b_im[l], c_re[l], c_im[l], d_skip[l], w_glu[l], b_glu[l],
                   w_out[l], ln_g[l], ln_b[l])
    return h
```
